```python
import math
import jax
import jax.numpy as jnp
from jax import lax
import numpy as np

D_MODEL = 1024
BATCH = 2
SEQ = 8192
DEPTH = 2
DEC_BATCH = 32
DEC_SEQ = 4
PAST_LEN = 16384
PAGE_SIZE = 128

HEAD_DIM = 64
ROPE_THETA = 10000.0
Q_BLOCK = 128
RMS_EPS = 1e-6

DIFF_HEADS = 4
DIFF_VDIM = 2 * HEAD_DIM
DIFF_WIDTH = DIFF_HEADS * DIFF_VDIM
MOBA_HEADS = 4
MOBA_BLOCK = 256
MOBA_TOPK = 3
MOBA_WIDTH = MOBA_HEADS * HEAD_DIM
NSA_HEADS = 4
NSA_KV_HEADS = 1
CMP_LEN = 32
CMP_STRIDE = 16
CMP_HIDDEN = 256
SEL_BLOCK = 64
SEL_TOPK = 16
WINDOW = 512
NSA_WIDTH = NSA_HEADS * HEAD_DIM

N_BRANCH = 3
SPLIT_SIZES = (
    DIFF_HEADS * 2 * HEAD_DIM,
    DIFF_HEADS * 2 * HEAD_DIM,
    DIFF_HEADS * DIFF_VDIM,
    MOBA_HEADS * HEAD_DIM,
    MOBA_HEADS * HEAD_DIM,
    MOBA_HEADS * HEAD_DIM,
    NSA_HEADS * HEAD_DIM,
    6 * NSA_KV_HEADS * HEAD_DIM,
    3 * NSA_HEADS,
    N_BRANCH * D_MODEL,
)
IN_COLS = sum(SPLIT_SIZES)

N_EXPERTS = 16
N_GROUPS = 4
EXPERTS_PER_GROUP = N_EXPERTS // N_GROUPS
MOE_TOPK = 2
EXPERT_FF = 512

kernel_name = 'hybrid_diff_moba_nsa_groupmoe_step'


def rmsnorm(x, g):
    xf = x.astype(jnp.float32)
    y = xf * lax.rsqrt(jnp.mean(xf * xf, axis=-1, keepdims=True) + RMS_EPS)
    return y.astype(x.dtype) * g


def rope(x, pos):
    half = x.shape[-1] // 2
    freqs = ROPE_THETA ** (-jnp.arange(half, dtype=jnp.float32) / half)
    ang = pos.astype(jnp.float32)[:, None] * freqs[None, :]
    shape = (1, pos.shape[0]) + (1,) * (x.ndim - 3) + (half,)
    cos = jnp.cos(ang).reshape(shape)
    sin = jnp.sin(ang).reshape(shape)
    x1 = x[..., :half].astype(jnp.float32)
    x2 = x[..., half:].astype(jnp.float32)
    return jnp.concatenate([x1 * cos - x2 * sin, x2 * cos + x1 * sin], axis=-1).astype(x.dtype)


def masked_softmax(s, mask):
    s = jnp.where(mask, s.astype(jnp.float32), -jnp.inf)
    m = jnp.max(s, axis=-1, keepdims=True)
    m = jnp.where(jnp.isfinite(m), m, 0.0)
    e = jnp.where(mask, jnp.exp(s - m), 0.0)
    den = jnp.sum(e, axis=-1, keepdims=True)
    return e / jnp.where(den > 0, den, 1.0)


def map_query_blocks(fn, q_arrays, q_pos):
    T = q_pos.shape[0]
    qb = min(Q_BLOCK, T)
    n = T // qb
    blocks = tuple(jnp.moveaxis(a.reshape(a.shape[0], n, qb, *a.shape[2:]), 1, 0) for a in q_arrays)
    out = lax.map(lambda args: fn(*args[0], args[1]), (blocks, q_pos.reshape(n, qb)))
    out = jnp.moveaxis(out, 0, 1)
    return out.reshape(out.shape[0], T, *out.shape[3:])


def adaln(c, w, b):
    mod = jax.nn.silu(c) @ w + b
    return jnp.split(mod, 6, axis=-1)


def modulate(x, g, shift, scale):
    return rmsnorm(x, g) * (1.0 + scale[:, None, :]) + shift[:, None, :]


def diff_lambda_value(lam_p, layer):
    lp = lam_p.astype(jnp.float32)
    lam_init = 0.8 - 0.6 * math.exp(-0.3 * layer)
    lam = jnp.exp(jnp.sum(lp[0] * lp[1])) - jnp.exp(jnp.sum(lp[2] * lp[3])) + lam_init
    return lam, lam_init


def project(h, w_in_l, pos):
    B, T, _ = h.shape
    dq, dk, dv, mq, mk, mv, nq, nkv, ng, bg = jnp.split(h @ w_in_l, np.cumsum(SPLIT_SIZES)[:-1].tolist(), axis=-1)
    dq = rope(dq.reshape(B, T, DIFF_HEADS, 2, HEAD_DIM), pos)
    dk = rope(dk.reshape(B, T, DIFF_HEADS, 2, HEAD_DIM), pos)
    dv = dv.reshape(B, T, DIFF_HEADS, DIFF_VDIM)
    mq = rope(mq.reshape(B, T, MOBA_HEADS, HEAD_DIM), pos)
    mk = rope(mk.reshape(B, T, MOBA_HEADS, HEAD_DIM), pos)
    mv = mv.reshape(B, T, MOBA_HEADS, HEAD_DIM)
    nq = rope(nq.reshape(B, T, NSA_HEADS, HEAD_DIM), pos)
    nkv = nkv.reshape(B, T, 6, NSA_KV_HEADS, HEAD_DIM)
    is_key = (jnp.arange(6) % 2 == 0)[None, None, :, None, None]
    nkv = jnp.where(is_key, rope(nkv, pos), nkv)
    ng = jax.nn.sigmoid(ng.reshape(B, T, NSA_HEADS, 3))
    bg = jax.nn.sigmoid(bg.reshape(B, T, N_BRANCH, D_MODEL))
    return dq, dk, dv, mq, mk, mv, nq, nkv, ng, bg


def diff_core(q, k, v, q_pos, k_pos, lam, lam_init, norm_g):
    B, Tq, H = q.shape[:3]
    s = jnp.einsum('bqhmd,bkhmd->bhmqk', q, k) * HEAD_DIM ** -0.5
    p = masked_softmax(s, (k_pos[None, :] <= q_pos[:, None])[None, None, None])
    a = p[:, :, 0] - lam * p[:, :, 1]
    o = jnp.einsum('bhqk,bkhe->bqhe', a.astype(v.dtype), v)
    return (rmsnorm(o, norm_g) * (1.0 - lam_init)).reshape(B, Tq, H * v.shape[-1])


def moba_core(q, k, v, q_pos):
    B, L, H, hd = k.shape
    Tq = q.shape[1]
    n_blk = -(-L // MOBA_BLOCK)
    pad = ((0, 0), (0, n_blk * MOBA_BLOCK - L), (0, 0), (0, 0))
    kb = jnp.pad(k, pad).reshape(B, n_blk, MOBA_BLOCK, H, hd)
    vb = jnp.pad(v, pad).reshape(B, n_blk, MOBA_BLOCK, H, hd)
    k_mean = jnp.mean(kb, axis=2)
    own = q_pos // MOBA_BLOCK
    gate = jnp.einsum('bqhd,bnhd->bqhn', q, k_mean).astype(jnp.float32)
    past = jnp.arange(n_blk)[None, :] < own[:, None]
    gate = jnp.where(past[None, :, None, :], gate, -jnp.inf)
    _, top_idx = lax.top_k(gate, min(MOBA_TOPK, n_blk))
    top_ok = top_idx < own[None, :, None, None]
    own_b = jnp.broadcast_to(own[None, :, None, None], (B, Tq, H, 1)).astype(top_idx.dtype)
    blk_idx = jnp.concatenate([top_idx, own_b], axis=-1)
    blk_ok = jnp.concatenate([top_ok, jnp.ones(own_b.shape, dtype=bool)], axis=-1)
    bi = jnp.arange(B)[:, None, None, None]
    hi = jnp.arange(H)[None, None, :, None]
    kg = jnp.transpose(kb, (0, 3, 1, 2, 4))[bi, hi, blk_idx]
    vg = jnp.transpose(vb, (0, 3, 1, 2, 4))[bi, hi, blk_idx]
    kpos = blk_idx[..., None] * MOBA_BLOCK + jnp.arange(MOBA_BLOCK)
    ok = blk_ok[..., None] & (kpos <= q_pos[None, :, None, None, None])
    s = jnp.einsum('bqhd,bqhnsd->bqhns', q, kg) * hd ** -0.5
    p = masked_softmax(s.reshape(B, Tq, H, -1), ok.reshape(B, Tq, H, -1))
    o = jnp.einsum('bqhm,bqhmd->bqhd', p.astype(v.dtype), vg.reshape(B, Tq, H, -1, hd))
    return o.reshape(B, Tq, H * hd)


def nsa_compress(k_raw, v_raw, pos_emb, w1, w2):
    B, L, KV, hd = k_raw.shape
    n_cmp = (L - CMP_LEN) // CMP_STRIDE + 1
    idx = jnp.arange(n_cmp)[:, None] * CMP_STRIDE + jnp.arange(CMP_LEN)[None, :]

    def phi(x, j):
        blk = x[:, idx] + pos_emb[j][None, None, :, None, :]
        flat = jnp.moveaxis(blk, 3, 2).reshape(B, n_cmp, KV, CMP_LEN * hd)
        return jax.nn.gelu(flat @ w1[j]) @ w2[j]

    return phi(k_raw, 0), phi(v_raw, 1)


def cmp_to_sel_weights(n_cmp, n_sel):
    ratio_sel = SEL_BLOCK // CMP_STRIDE
    ratio_cmp = CMP_LEN // CMP_STRIDE
    off = jnp.arange(n_cmp)[:, None] - ratio_sel * jnp.arange(n_sel)[None, :]
    shifts = (jnp.arange(ratio_sel)[:, None] - jnp.arange(ratio_cmp)[None, :]).reshape(-1)
    return jnp.sum(off[..., None] == shifts, axis=-1).astype(jnp.float32)


def nsa_cmp_sel(q, k_cmp, v_cmp, k_sel, v_sel, q_pos):
    B, Tq, H, hd = q.shape
    L, KV = k_sel.shape[1], k_sel.shape[2]
    qg = q.reshape(B, Tq, KV, H // KV, hd)
    scale = hd ** -0.5
    n_cmp = k_cmp.shape[1]
    cmp_end = jnp.arange(n_cmp) * CMP_STRIDE + (CMP_LEN - 1)
    cmp_ok = (cmp_end[None, :] <= q_pos[:, None])[None, :, None, None, :]
    p_cmp = masked_softmax(jnp.einsum('bqkgd,bnkd->bqkgn', qg, k_cmp) * scale, cmp_ok)
    o_cmp = jnp.einsum('bqkgn,bnkd->bqkgd', p_cmp.astype(v_cmp.dtype), v_cmp)
    n_sel = -(-L // SEL_BLOCK)
    imp = jnp.einsum('bqkgn,nj->bqkj', p_cmp, cmp_to_sel_weights(n_cmp, n_sel))
    own = q_pos // SEL_BLOCK
    j = jnp.arange(n_sel)[None, :]
    valid = (j <= own[:, None])[:, None, :]
    forced = ((j == 0) | (j == own[:, None]) | (j == own[:, None] - 1))[:, None, :]
    imp = jnp.where(valid, jnp.where(forced, jnp.inf, imp), -jnp.inf)
    _, top_idx = lax.top_k(imp, min(SEL_TOPK, n_sel))
    top_ok = top_idx <= own[None, :, None, None]
    pad = ((0, 0), (0, n_sel * SEL_BLOCK - L), (0, 0), (0, 0))
    bi = jnp.arange(B)[:, None, None, None]
    ki = jnp.arange(KV)[None, None, :, None]

    def gather_blocks(x):
        xb = jnp.pad(x, pad).reshape(B, n_sel, SEL_BLOCK, KV, hd)
        return jnp.transpose(xb, (0, 3, 1, 2, 4))[bi, ki, top_idx]

    kg = gather_blocks(k_sel)
    vg = gather_blocks(v_sel)
    kpos = top_idx[..., None] * SEL_BLOCK + jnp.arange(SEL_BLOCK)
    ok = top_ok[..., None] & (kpos <= q_pos[None, :, None, None, None])
    s = jnp.einsum('bqkgd,bqkmsd->bqkgms', qg, kg) * scale
    p = masked_softmax(s.reshape(B, Tq, KV, H // KV, -1), ok.reshape(B, Tq, KV, 1, -1))
    o_sel = jnp.einsum('bqkgm,bqkmd->bqkgd', p.astype(vg.dtype), vg.reshape(B, Tq, KV, -1, hd))
    return o_cmp.reshape(B, Tq, H, hd), o_sel.reshape(B, Tq, H, hd)


def window_core(q, k, v, q_pos, k_pos):
    B, Tq, H, hd = q.shape
    KV = k.shape[2]
    qg = q.reshape(B, Tq, KV, H // KV, hd)
    s = jnp.einsum('bqkgd,bskd->bqkgs', qg, k) * hd ** -0.5
    ok = (k_pos[None, :] <= q_pos[:, None]) & (k_pos[None, :] > q_pos[:, None] - WINDOW)
    p = masked_softmax(s, ok[None, :, None, None, :])
    return jnp.einsum('bqkgs,bskd->bqkgd', p.astype(v.dtype), v).reshape(B, Tq, H, hd)


def nsa_combine(g, o_cmp, o_sel, o_win):
    o = g[..., 0:1] * o_cmp + g[..., 1:2] * o_sel + g[..., 2:3] * o_win
    return o.reshape(o.shape[0], o.shape[1], -1)


def merge(o_a, o_b, o_c, bg, w_br_a, w_br_b, w_br_c, w_out):
    m = bg[:, :, 0] * (o_a @ w_br_a) + bg[:, :, 1] * (o_b @ w_br_b) + bg[:, :, 2] * (o_c @ w_br_c)
    return m @ w_out


def mixer_prompt(h, w_in_l, lam, lam_init, diff_g, cmp_params, merge_params):
    T = h.shape[1]
    pos = jnp.arange(T, dtype=jnp.int32)
    dq, dk, dv, mq, mk, mv, nq, nkv, ng, bg = project(h, w_in_l, pos)
    o_a = map_query_blocks(lambda q, p: diff_core(q, dk, dv, p, pos, lam, lam_init, diff_g), (dq,), pos)
    o_b = map_query_blocks(lambda q, p: moba_core(q, mk, mv, p), (mq,), pos)
    k_cmp, v_cmp = nsa_compress(nkv[:, :, 0], nkv[:, :, 1], *cmp_params)
    k_sel, v_sel = nkv[:, :, 2], nkv[:, :, 3]
    wpad = ((0, 0), (WINDOW, 0), (0, 0), (0, 0))
    k_win = jnp.pad(nkv[:, :, 4], wpad)
    v_win = jnp.pad(nkv[:, :, 5], wpad)

    def nsa_block(q, g, p):
        o_cmp, o_sel = nsa_cmp_sel(q, k_cmp, v_cmp, k_sel, v_sel, p)
        n_keys = WINDOW + p.shape[0]
        kw = lax.dynamic_slice_in_dim(k_win, p[0], n_keys, axis=1)
        vw = lax.dynamic_slice_in_dim(v_win, p[0], n_keys, axis=1)
        o_win = window_core(q, kw, vw, p, p[0] - WINDOW + jnp.arange(n_keys, dtype=jnp.int32))
        return nsa_combine(g, o_cmp, o_sel, o_win)

    o_c = map_query_blocks(nsa_block, (nq, ng), pos)
    out = merge(o_a, o_b, o_c, bg, *merge_params)
    keep = min(WINDOW, T)
    diff_rows = jnp.stack([dk.reshape(*dk.shape[:3], -1), dv], axis=2)
    moba_rows = jnp.stack([mk, mv], axis=2)
    return out, diff_rows, moba_rows, nkv[:, :, :4], nkv[:, T - keep:, 4:]


def mixer_sample(h, w_in_l, lam, lam_init, diff_g, cmp_params, merge_params, layer,
                 cache_diff_kv, cache_moba_kv, cache_nsa_kv, win_state, page_table):
    S = h.shape[1]
    past = page_table.shape[1] * PAGE_SIZE
    total = past + S
    keep = win_state.shape[1]
    pos = past + jnp.arange(S, dtype=jnp.int32)
    k_pos = jnp.arange(total, dtype=jnp.int32)
    w_pos = past - keep + jnp.arange(keep + S, dtype=jnp.int32)
    dq, dk, dv, mq, mk, mv, nq, nkv, ng, bg = project(h, w_in_l, pos)

    def one_sequence(args):
        dq_i, dk_i, dv_i, mq_i, mk_i, mv_i, nq_i, nkv_i, ng_i, win_i, pages = args
        d_past = cache_diff_kv[layer, pages].reshape(past, 2, DIFF_HEADS, DIFF_VDIM)
        kd = jnp.concatenate([d_past[:, 0].reshape(past, DIFF_HEADS, 2, HEAD_DIM), dk_i], axis=0)
        vd = jnp.concatenate([d_past[:, 1], dv_i], axis=0)
        o_a = diff_core(dq_i[None], kd[None], vd[None], pos, k_pos, lam, lam_init, diff_g)
        m_past = cache_moba_kv[layer, pages].reshape(past, 2, MOBA_HEADS, HEAD_DIM)
        km = jnp.concatenate([m_past[:, 0], mk_i], axis=0)
        vm = jnp.concatenate([m_past[:, 1], mv_i], axis=0)
        o_b = moba_core(mq_i[None], km[None], vm[None], pos)
        n_past = cache_nsa_kv[layer, pages].reshape(past, 4, NSA_KV_HEADS, HEAD_DIM)
        n_all = jnp.concatenate([n_past, nkv_i[:, :4]], axis=0)[None]
        k_cmp, v_cmp = nsa_compress(n_all[:, :, 0], n_all[:, :, 1], *cmp_params)
        o_cmp, o_sel = nsa_cmp_sel(nq_i[None], k_cmp, v_cmp, n_all[:, :, 2], n_all[:, :, 3], pos)
        w_all = jnp.concatenate([win_i, nkv_i[:, 4:]], axis=0)
        o_win = window_core(nq_i[None], w_all[None, :, 0], w_all[None, :, 1], pos, w_pos)
        o_c = nsa_combine(ng_i[None], o_cmp, o_sel, o_win)
        return o_a[0], o_b[0], o_c[0], w_all[S:]

    o_a, o_b, o_c, new_win = lax.map(one_sequence, (dq, dk, dv, mq, mk, mv, nq, nkv, ng, win_state, page_table))
    out = merge(o_a, o_b, o_c, bg, *merge_params)
    diff_rows = jnp.stack([dk.reshape(*dk.shape[:3], -1), dv], axis=2)
    moba_rows = jnp.stack([mk, mv], axis=2)
    return out, diff_rows, moba_rows, nkv[:, :, :4], new_win


def moe(h, router_w, router_b, w_gate, w_up, w_down):
    aff = jax.nn.sigmoid(jnp.einsum('btd,de->bte', h, router_w).astype(jnp.float32))
    biased = aff + router_b.astype(jnp.float32)
    grouped = biased.reshape(*biased.shape[:-1], N_GROUPS, EXPERTS_PER_GROUP)
    group_score = jnp.sum(lax.top_k(grouped, MOE_TOPK)[0], axis=-1)
    g_idx = jnp.argmax(group_score, axis=-1)
    g_sel = jnp.broadcast_to(g_idx[..., None, None], grouped.shape[:-2] + (1, EXPERTS_PER_GROUP))
    in_group = jnp.take_along_axis(grouped, g_sel, axis=-2)[..., 0, :]
    _, local = lax.top_k(in_group, MOE_TOPK)
    e_idx = g_idx[..., None] * EXPERTS_PER_GROUP + local
    w = jnp.take_along_axis(aff, e_idx, axis=-1)
    w = w / jnp.sum(w, axis=-1, keepdims=True)
    gate = jnp.einsum('btk,btke->bte', w, jax.nn.one_hot(e_idx, N_EXPERTS, dtype=jnp.float32))
    hid = jax.nn.silu(jnp.einsum('btd,edf->btef', h, w_gate)) * jnp.einsum('btd,edf->btef', h, w_up)
    hid = hid * gate.astype(h.dtype)[..., None]
    return jnp.einsum('btef,efd->btd', hid, w_down)


def setup_inputs(seed: int = 0) -> dict:
    key = jax.random.key(seed)
    ks = jax.random.split(key, 32)
    D = D_MODEL
    n_pages = PAST_LEN // PAGE_SIZE
    n_used = DEC_BATCH * n_pages
    n_phys = n_used + max(1, n_used // 4)
    keep = min(WINDOW, PAST_LEN)

    def nrm(k, shape, scale=1.0):
        return jax.random.normal(k, shape, jnp.float32) * scale

    page_table = jax.random.permutation(ks[6], n_phys)[:n_used].reshape(DEC_BATCH, n_pages).astype(jnp.int32)
    return {
        'x_prompt': nrm(ks[0], (BATCH, SEQ, D)),
        'x_sample': nrm(ks[1], (DEC_BATCH, DEC_SEQ, D)),
        'cache_diff_kv': nrm(ks[2], (DEPTH, n_phys, PAGE_SIZE, 2, DIFF_HEADS, DIFF_VDIM)),
        'cache_moba_kv': nrm(ks[3], (DEPTH, n_phys, PAGE_SIZE, 2, MOBA_HEADS, HEAD_DIM)),
        'cache_nsa_kv': nrm(ks[4], (DEPTH, n_phys, PAGE_SIZE, 4, NSA_KV_HEADS, HEAD_DIM)),
        'state_nsa_win': nrm(ks[5], (DEPTH, DEC_BATCH, keep, 2, NSA_KV_HEADS, HEAD_DIM)),
        'page_table': page_table,
        'c_prompt': nrm(ks[7], (BATCH, D)),
        'c_sample': nrm(ks[8], (DEC_BATCH, D)),
        'w_ada': nrm(ks[9], (DEPTH, D, 6 * D), 0.5 * D ** -0.5),
        'b_ada': nrm(ks[10], (DEPTH, 6 * D), 0.02),
        'g_mix': 1.0 + nrm(ks[11], (DEPTH, D), 0.05),
        'g_ffn': 1.0 + nrm(ks[12], (DEPTH, D), 0.05),
        'w_in': nrm(ks[13], (DEPTH, D, IN_COLS), D ** -0.5),
        'diff_lambda': nrm(ks[14], (DEPTH, 4, HEAD_DIM), 0.1),
        'diff_norm_g': 1.0 + nrm(ks[15], (DEPTH, DIFF_VDIM), 0.05),
        'cmp_pos': nrm(ks[16], (DEPTH, 2, CMP_LEN, HEAD_DIM), 0.1),
        'cmp_w1': nrm(ks[17], (DEPTH, 2, CMP_LEN * HEAD_DIM, CMP_HIDDEN), (CMP_LEN * HEAD_DIM) ** -0.5),
        'cmp_w2': nrm(ks[18], (DEPTH, 2, CMP_HIDDEN, HEAD_DIM), CMP_HIDDEN ** -0.5),
        'w_br_a': nrm(ks[19], (DEPTH, DIFF_WIDTH, D), DIFF_WIDTH ** -0.5),
        'w_br_b': nrm(ks[20], (DEPTH, MOBA_WIDTH, D), MOBA_WIDTH ** -0.5),
        'w_br_c': nrm(ks[21], (DEPTH, NSA_WIDTH, D), NSA_WIDTH ** -0.5),
        'w_out': nrm(ks[22], (DEPTH, D, D), D ** -0.5),
        'router_w': nrm(ks[23], (D, N_EXPERTS), D ** -0.5),
        'router_b': nrm(ks[24], (N_EXPERTS,), 0.01),
        'w_e_gate': nrm(ks[25], (DEPTH, N_EXPERTS, D, EXPERT_FF), D ** -0.5),
        'w_e_up': nrm(ks[26], (DEPTH, N_EXPERTS, D, EXPERT_FF), D ** -0.5),
        'w_e_down': nrm(ks[27], (DEPTH, N_EXPERTS, EXPERT_FF, D), EXPERT_FF ** -0.5),
        'g_final': 1.0 + nrm(ks[28], (D,), 0.05),
    }


def reference(x_prompt, x_sample, cache_diff_kv, cache_moba_kv, cache_nsa_kv, state_nsa_win, page_table,
              c_prompt, c_sample, w_ada, b_ada, g_mix, g_ffn, w_in, diff_lambda, diff_norm_g,
              cmp_pos, cmp_w1, cmp_w2, w_br_a, w_br_b, w_br_c, w_out, router_w, router_b,
              w_e_gate, w_e_up, w_e_down, g_final):
    xp, xs = x_prompt, x_sample
    d_p, d_s, m_p, m_s, n_p, n_s, w_p, w_s = [], [], [], [], [], [], [], []
    for l in range(DEPTH):
        lam, lam_init = diff_lambda_value(diff_lambda[l], l)
        cmp_params = (cmp_pos[l], cmp_w1[l], cmp_w2[l])
        merge_params = (w_br_a[l], w_br_b[l], w_br_c[l], w_out[l])
        moe_params = (router_w, router_b, w_e_gate[l], w_e_up[l], w_e_down[l])
        sh1, sc1, gt1, sh2, sc2, gt2 = adaln(c_prompt, w_ada[l], b_ada[l])
        mix, d_rows, m_rows, n_rows, w_rows = mixer_prompt(modulate(xp, g_mix[l], sh1, sc1), w_in[l], lam, lam_init,
                                                           diff_norm_g[l], cmp_params, merge_params)
        xp = xp + gt1[:, None, :] * mix
        xp = xp + gt2[:, None, :] * moe(modulate(xp, g_ffn[l], sh2, sc2), *moe_params)
        d_p.append(d_rows)
        m_p.append(m_rows)
        n_p.append(n_rows)
        w_p.append(w_rows)
        sh1, sc1, gt1, sh2, sc2, gt2 = adaln(c_sample, w_ada[l], b_ada[l])
        mix, d_rows, m_rows, n_rows, w_new = mixer_sample(modulate(xs, g_mix[l], sh1, sc1), w_in[l], lam, lam_init,
                                                          diff_norm_g[l], cmp_params, merge_params, l,
                                                          cache_diff_kv, cache_moba_kv, cache_nsa_kv,
                                                          state_nsa_win[l], page_table)
        xs = xs + gt1[:, None, :] * mix
        xs = xs + gt2[:, None, :] * moe(modulate(xs, g_ffn[l], sh2, sc2), *moe_params)
        d_s.append(d_rows)
        m_s.append(m_rows)
        n_s.append(n_rows)
        w_s.append(w_new)
    y_prompt = rmsnorm(xp, g_final)
    y_sample = rmsnorm(xs, g_final)
    return (y_prompt, y_sample, jnp.stack(d_p), jnp.stack(d_s), jnp.stack(m_p), jnp.stack(m_s),
            jnp.stack(n_p), jnp.stack(n_s), jnp.stack(w_p), jnp.stack(w_s))
```

```python
import functools
import math

import jax
import jax.numpy as jnp
import numpy as np
from jax import lax
from jax.experimental import pallas as pl
from jax.experimental.pallas import tpu as pltpu

D_MODEL = 1024
HEAD_DIM = 64
HALF = HEAD_DIM // 2
ROPE_THETA = 10000.0
RMS_EPS = 1e-6
PAGE_SIZE = 128
DIFF_HEADS = 4
DIFF_VDIM = 2 * HEAD_DIM
MOBA_HEADS = 4
MOBA_BLOCK = 256
MOBA_TOPK = 3
NSA_HEADS = 4
CMP_LEN = 32
CMP_STRIDE = 16
CMP_HIDDEN = 256
SEL_BLOCK = 64
SEL_TOPK = 16
WINDOW = 512
N_BRANCH = 3
N_NEW = 4
N_EXPERTS = 16
N_GROUPS = 4
EXPERTS_PER_GROUP = N_EXPERTS // N_GROUPS
EXPERT_FF = 512
PROJ_COLS = 2956
PROJ_PAD = 3072
Q_SCALE = HEAD_DIM ** -0.5

LANES = 128
VMEM_LIMIT_BYTES = 56 * 1024 * 1024

MXU_DTYPE = jnp.bfloat16
F32 = jnp.float32
NEG = -1e30


def _cparams(*sem):
    return pltpu.CompilerParams(dimension_semantics=sem, vmem_limit_bytes=VMEM_LIMIT_BYTES)


def _sigmoid(x):
    return 1.0 / (1.0 + jnp.exp(-x))


def _silu(x):
    return x * _sigmoid(x)


def _gelu_tanh(x):
    c = math.sqrt(2.0 / math.pi)
    return 0.5 * x * (1.0 + jnp.tanh(c * (x + 0.044715 * (x * x * x))))


def _dot(a, b):
    return jnp.dot(a.astype(MXU_DTYPE), b.astype(MXU_DTYPE), preferred_element_type=F32)


def _dot_nt(a, b):
    return lax.dot_general(a.astype(MXU_DTYPE), b.astype(MXU_DTYPE), (((1,), (1,)), ((), ())),
                           preferred_element_type=F32)


def _rms(x):
    return x * lax.rsqrt(jnp.mean(x * x, axis=-1, keepdims=True) + RMS_EPS)


def _mod_norm(x, g, shift, scale):
    return (_rms(x) * g) * (1.0 + scale) + shift


def _lane(shape, dim=None):
    return lax.broadcasted_iota(jnp.int32, shape, len(shape) - 1 if dim is None else dim)


def _rope(y, cos_t, sin_t):
    outs = []
    for c in range(y.shape[1] // LANES):
        blk = y[:, c * LANES:(c + 1) * LANES]
        first = (_lane(blk.shape) % HEAD_DIM) < HALF
        partner = jnp.where(first, pltpu.roll(blk, LANES - HALF, 1), pltpu.roll(blk, HALF, 1))
        outs.append(blk * cos_t + partner * sin_t)
    return outs[0] if len(outs) == 1 else jnp.concatenate(outs, axis=1)


def _flash_first(s, v, m_ref, l_ref, acc_ref):
    m = jnp.max(s, axis=-1, keepdims=True)
    p = jnp.exp(s - m)
    m_ref[...] = m
    l_ref[...] = jnp.sum(p, axis=-1, keepdims=True)
    acc_ref[...] = _dot(p, v)


def _flash_next(s, v, m_ref, l_ref, acc_ref):
    m_old = m_ref[...]
    m = jnp.maximum(m_old, jnp.max(s, axis=-1, keepdims=True))
    alpha = jnp.exp(m_old - m)
    p = jnp.exp(s - m)
    m_ref[...] = m
    l_ref[...] = alpha * l_ref[...] + jnp.sum(p, axis=-1, keepdims=True)
    acc_ref[...] = alpha * acc_ref[...] + _dot(p, v)


def _top_k_mask(vals, k, n_valid_lanes):
    lane = _lane(vals.shape)
    big = jnp.int32(vals.shape[-1])

    def body(_, carry):
        sel, cur = carry
        m = jnp.max(cur, axis=-1, keepdims=True)
        idx = jnp.min(jnp.where(cur == m, lane, big), axis=-1, keepdims=True)
        hit = lane == idx
        return jnp.where(hit, 1.0, sel), jnp.where(hit, -jnp.inf, cur)

    cur0 = jnp.where(lane < n_valid_lanes, vals, -jnp.inf)
    sel, _ = lax.fori_loop(0, k, body, (jnp.zeros(vals.shape, F32), cur0))
    return sel


def _ada_kernel(c_ref, w_ref, b_ref, o_ref):
    o_ref[0] = _dot(_silu(c_ref[...]), w_ref[0]) + b_ref[0]


def ada_ln(c_all, w_ada, b_ada):
    depth, _, n = w_ada.shape
    rc = c_all.shape[0]
    tn = 1536
    return pl.pallas_call(
        _ada_kernel,
        out_shape=jax.ShapeDtypeStruct((depth, rc, n), F32),
        grid=(depth, n // tn),
        in_specs=[pl.BlockSpec((rc, D_MODEL), lambda l, j: (0, 0)),
                  pl.BlockSpec((1, D_MODEL, tn), lambda l, j: (l, 0, j)),
                  pl.BlockSpec((1, 1, tn), lambda l, j: (l, 0, j))],
        out_specs=pl.BlockSpec((1, rc, tn), lambda l, j: (l, 0, j)),
        compiler_params=_cparams("arbitrary", "arbitrary"),
        name="ada_ln",
    )(c_all, w_ada, b_ada.reshape(depth, 1, n))


def _mod_spec(per_token, tm, rows_per_batch_tile, chunk):
    if per_token:
        return pl.BlockSpec((tm, D_MODEL), lambda i: (i, chunk))
    return pl.BlockSpec((None, 1, D_MODEL), lambda i: (i // rows_per_batch_tile, 0, chunk))


def _proj_kernel(x_ref, g_ref, sh_ref, sc_ref, w_ref, cf_ref, sf_ref, ck_ref, sk_ref,
                 dq_ref, dkv_ref, mq_ref, mkv_ref, nq_ref, n4_ref, nwin_ref, ng_ref):
    h = _mod_norm(x_ref[...], g_ref[0], sh_ref[...], sc_ref[...])
    y = _dot(h, w_ref[0])
    cf, sf, ck, sk = cf_ref[...], sf_ref[...], ck_ref[...], sk_ref[...]
    dq_ref[...] = (_rope(y[:, 0:512], cf, sf) * Q_SCALE).astype(dq_ref.dtype)
    dkv_ref[:, 0:512] = _rope(y[:, 512:1024], cf, sf)
    dkv_ref[:, 512:1024] = y[:, 1024:1536]
    mq_ref[...] = (_rope(y[:, 1536:1792], cf, sf) * Q_SCALE).astype(mq_ref.dtype)
    mkv_ref[:, 0:256] = _rope(y[:, 1792:2048], cf, sf)
    mkv_ref[:, 256:512] = y[:, 2048:2304]
    nq_ref[...] = _rope(y[:, 2304:2560], cf, sf) * Q_SCALE
    n4_ref[...] = _rope(y[:, 2560:2816], ck, sk)
    nwin_ref[...] = _rope(y[:, 2816:2944], ck, sk)
    ng_ref[...] = _sigmoid(y[:, 2944:3072])


def project(x, layer, g_mix, shift, scale, w1, tables, *, tm, rows_per_batch, per_token):
    r = x.shape[0]
    nt = rows_per_batch // tm
    row = lambda w: pl.BlockSpec((tm, w), lambda i: (i, 0))
    tab = pl.BlockSpec((tm, LANES), lambda i: (i % nt, 0))
    out_shapes = (
        jax.ShapeDtypeStruct((r, 512), MXU_DTYPE),
        jax.ShapeDtypeStruct((r, 1024), F32),
        jax.ShapeDtypeStruct((r, 256), MXU_DTYPE),
        jax.ShapeDtypeStruct((r, 512), F32),
        jax.ShapeDtypeStruct((r, 256), F32),
        jax.ShapeDtypeStruct((r, 256), F32),
        jax.ShapeDtypeStruct((r, 128), F32),
        jax.ShapeDtypeStruct((r, 128), F32),
    )
    return pl.pallas_call(
        _proj_kernel,
        out_shape=out_shapes,
        grid=(r // tm,),
        in_specs=[row(D_MODEL),
                  pl.BlockSpec((1, 1, D_MODEL), lambda i: (layer, 0, 0)),
                  _mod_spec(per_token, tm, nt, 0), _mod_spec(per_token, tm, nt, 1),
                  pl.BlockSpec((1, D_MODEL, PROJ_PAD), lambda i: (layer, 0, 0)),
                  tab, tab, tab, tab],
        out_specs=tuple(row(s.shape[1]) for s in out_shapes),
        compiler_params=_cparams("arbitrary"),
        name="project",
    )(x, g_mix, shift, scale, w1, *tables)


def rope_tables(pos):
    freqs = ROPE_THETA ** (-jnp.arange(HALF, dtype=F32) / HALF)
    ang = pos.astype(F32)[:, None] * freqs[None, :]
    cos, sin = jnp.cos(ang), jnp.sin(ang)
    c64 = jnp.concatenate([cos, cos], axis=1)
    s64 = jnp.concatenate([-sin, sin], axis=1)
    one, zero = jnp.ones_like(c64), jnp.zeros_like(s64)
    return (jnp.concatenate([c64, c64], axis=1), jnp.concatenate([s64, s64], axis=1),
            jnp.concatenate([c64, one], axis=1), jnp.concatenate([s64, zero], axis=1))


def _diff_lambda(lam_ref, lam_init):
    lp = lam_ref[0]
    a = jnp.sum(lp[0:1] * lp[1:2], axis=-1, keepdims=True)
    b = jnp.sum(lp[2:3] * lp[3:4], axis=-1, keepdims=True)
    return jnp.exp(a) - jnp.exp(b) + lam_init


def _diff_prompt_kernel(q_ref, k_ref, v_ref, lam_ref, g_ref, o_ref,
                        kb, vb, m1, l1, a1, m2, l2, a2, *, blk, lam_init):
    i = pl.program_id(2)

    @pl.when(i == 0)
    def _():
        kb[...] = k_ref[...].astype(kb.dtype)
        vb[...] = v_ref[...].astype(vb.dtype)

    q = q_ref[...]
    first_half = _lane(q.shape) < HEAD_DIM
    q1 = jnp.where(first_half, q, jnp.zeros_like(q))
    q2 = jnp.where(first_half, jnp.zeros_like(q), q)
    maps = ((q1, m1, l1, a1), (q2, m2, l2, a2))

    d0 = pl.multiple_of(i * blk, blk)
    kd, vd = kb[pl.ds(d0, blk), :], vb[pl.ds(d0, blk), :]
    causal = _lane((blk, blk)) <= _lane((blk, blk), 0)
    for qm, m_ref, l_ref, a_ref in maps:
        _flash_first(jnp.where(causal, _dot_nt(qm, kd), NEG), vd, m_ref, l_ref, a_ref)

    def body(j, carry):
        j0 = pl.multiple_of(j * blk, blk)
        kj, vj = kb[pl.ds(j0, blk), :], vb[pl.ds(j0, blk), :]
        for qm, m_ref, l_ref, a_ref in maps:
            _flash_next(_dot_nt(qm, kj), vj, m_ref, l_ref, a_ref)
        return carry

    lax.fori_loop(0, i, body, 0)

    lam = _diff_lambda(lam_ref, lam_init)
    o = a1[...] / l1[...] - lam * (a2[...] / l2[...])
    o_ref[...] = (_rms(o) * g_ref[0] * (1.0 - lam_init)).astype(o_ref.dtype)


def diff_prompt(dq, dkv, diff_lambda, diff_norm_g, layer, *, batch, seq, blk=256):
    nq = seq // blk
    lam_init = 0.8 - 0.6 * math.exp(-0.3 * layer)
    kern = functools.partial(_diff_prompt_kernel, blk=blk, lam_init=lam_init)
    stat = pltpu.VMEM((blk, 1), F32)
    acc = pltpu.VMEM((blk, DIFF_VDIM), F32)
    kv = pltpu.VMEM((seq, DIFF_VDIM), MXU_DTYPE)
    return pl.pallas_call(
        kern,
        out_shape=jax.ShapeDtypeStruct((batch * seq, DIFF_HEADS * DIFF_VDIM), MXU_DTYPE),
        grid=(batch, DIFF_HEADS, nq),
        in_specs=[pl.BlockSpec((blk, 128), lambda b, h, i: (b * nq + i, h)),
                  pl.BlockSpec((seq, 128), lambda b, h, i: (b, h)),
                  pl.BlockSpec((seq, 128), lambda b, h, i: (b, DIFF_HEADS + h)),
                  pl.BlockSpec((1, 4, HEAD_DIM), lambda b, h, i: (layer, 0, 0)),
                  pl.BlockSpec((1, 1, DIFF_VDIM), lambda b, h, i: (layer, 0, 0))],
        out_specs=pl.BlockSpec((blk, 128), lambda b, h, i: (b * nq + i, h)),
        scratch_shapes=[kv, kv, stat, stat, acc, stat, stat, acc],
        compiler_params=_cparams("arbitrary", "arbitrary", "arbitrary"),
        name="diff_prompt",
    )(dq, dkv, dkv, diff_lambda, diff_norm_g)


def _moba_prompt_kernel(q_ref, k_ref, v_ref, o_ref, kb, vb, kmean, m1, l1, a1, m2, l2, a2, *, n_blk):
    i = pl.program_id(2)
    blk = MOBA_BLOCK

    @pl.when(i == 0)
    def _():
        kb[...] = k_ref[...].astype(kb.dtype)
        vb[...] = v_ref[...].astype(vb.dtype)
        for n in range(n_blk):
            kmean[n:n + 1, :] = jnp.sum(k_ref[n * blk:(n + 1) * blk, :], axis=0, keepdims=True) * (1.0 / blk)

    q = q_ref[...]
    first_half = _lane(q.shape) < HEAD_DIM
    q1 = jnp.where(first_half, q, jnp.zeros_like(q))
    q2 = jnp.where(first_half, jnp.zeros_like(q), q)
    km = kmean[...]
    maps = []
    for qm, m_ref, l_ref, a_ref in ((q1, m1, l1, a1), (q2, m2, l2, a2)):
        gate = _dot_nt(qm, km)
        blk_id = _lane(gate.shape)
        sel = _top_k_mask(jnp.where(blk_id < i, gate, -jnp.inf), MOBA_TOPK, n_blk)
        sel = jnp.where(blk_id < i, sel, 0.0)
        maps.append((qm, sel, m_ref, l_ref, a_ref))

    d0 = pl.multiple_of(i * blk, blk)
    kd, vd = kb[pl.ds(d0, blk), :], vb[pl.ds(d0, blk), :]
    causal = _lane((blk, blk)) <= _lane((blk, blk), 0)
    for qm, sel, m_ref, l_ref, a_ref in maps:
        _flash_first(jnp.where(causal, _dot_nt(qm, kd), NEG), vd, m_ref, l_ref, a_ref)

    def body(j, carry):
        j0 = pl.multiple_of(j * blk, blk)
        kj, vj = kb[pl.ds(j0, blk), :], vb[pl.ds(j0, blk), :]
        for qm, sel, m_ref, l_ref, a_ref in maps:
            on = jnp.sum(jnp.where(_lane(sel.shape) == j, sel, 0.0), axis=-1, keepdims=True) > 0.5
            _flash_next(jnp.where(on, _dot_nt(qm, kj), NEG), vj, m_ref, l_ref, a_ref)
        return carry

    lax.fori_loop(0, i, body, 0)
    o = jnp.where(first_half, a1[...] / l1[...], a2[...] / l2[...])
    o_ref[...] = o.astype(o_ref.dtype)


def moba_prompt(mq, mkv, *, batch, seq):
    blk = MOBA_BLOCK
    nq = seq // blk
    kern = functools.partial(_moba_prompt_kernel, n_blk=nq)
    stat = pltpu.VMEM((blk, 1), F32)
    acc = pltpu.VMEM((blk, 128), F32)
    kv = pltpu.VMEM((seq, 128), MXU_DTYPE)
    pairs = MOBA_HEADS // 2
    return pl.pallas_call(
        kern,
        out_shape=jax.ShapeDtypeStruct((batch * seq, MOBA_HEADS * HEAD_DIM), MXU_DTYPE),
        grid=(batch, pairs, nq),
        in_specs=[pl.BlockSpec((blk, 128), lambda b, h, i: (b * nq + i, h)),
                  pl.BlockSpec((seq, 128), lambda b, h, i: (b, h)),
                  pl.BlockSpec((seq, 128), lambda b, h, i: (b, pairs + h))],
        out_specs=pl.BlockSpec((blk, 128), lambda b, h, i: (b * nq + i, h)),
        scratch_shapes=[kv, kv, pltpu.VMEM((nq, 128), F32), stat, stat, acc, stat, stat, acc],
        compiler_params=_cparams("arbitrary", "arbitrary", "arbitrary"),
        name="moba_prompt",
    )(mq, mkv, mkv)


def compress_weights(cmp_pos, cmp_w1, cmp_w2):
    depth = cmp_w1.shape[0]
    w1k = cmp_w1[:, 0].reshape(depth, CMP_LEN, HEAD_DIM, CMP_HIDDEN)
    w1v = cmp_w1[:, 1].reshape(depth, CMP_LEN, HEAD_DIM, CMP_HIDDEN)
    z = jnp.zeros_like(w1k)
    per_row = jnp.concatenate([jnp.concatenate([w1k, z], axis=-1), jnp.concatenate([z, w1v], axis=-1)], axis=2)
    w1p = per_row.reshape(depth, CMP_LEN // 2, 4 * HEAD_DIM, 2 * CMP_HIDDEN).astype(MXU_DTYPE)
    posrows = jnp.transpose(cmp_pos, (0, 2, 1, 3)).reshape(depth, CMP_LEN, 1, 2 * HEAD_DIM)
    z2 = jnp.zeros_like(cmp_w2[:, 0])
    w2p = jnp.concatenate([jnp.concatenate([cmp_w2[:, 0], z2], axis=-1),
                           jnp.concatenate([z2, cmp_w2[:, 1]], axis=-1)], axis=1).astype(MXU_DTYPE)
    return posrows, w1p, w2p


def _compress_rows(xs, n_tok, pos_ref, w1_ref, w2_ref):
    acc = jnp.zeros((n_tok, 2 * CMP_HIDDEN), F32)
    for p in range(CMP_LEN // 2):
        xa = xs[pl.ds(2 * p, n_tok, stride=CMP_STRIDE), :] + pos_ref[0, 2 * p]
        xb = xs[pl.ds(2 * p + 1, n_tok, stride=CMP_STRIDE), :] + pos_ref[0, 2 * p + 1]
        acc = acc + _dot(jnp.concatenate([xa, xb], axis=1), w1_ref[0, p])
    return _dot(_gelu_tanh(acc), w2_ref[0])


def _compress_prompt_kernel(x_ref, pos_ref, w1_ref, w2_ref, o_ref, xs, *, seq, n_tok):
    xs[0:seq, :] = x_ref[...]
    xs[seq:seq + CMP_LEN, :] = jnp.zeros((CMP_LEN, 128), F32)
    o_ref[...] = _compress_rows(xs, n_tok, pos_ref, w1_ref, w2_ref)


def compress_prompt(n4, posrows, w1p, w2p, layer, *, batch, seq):
    n_tok = seq // CMP_STRIDE
    kern = functools.partial(_compress_prompt_kernel, seq=seq, n_tok=n_tok)
    return pl.pallas_call(
        kern,
        out_shape=jax.ShapeDtypeStruct((batch, n_tok, 128), F32),
        grid=(batch,),
        in_specs=[pl.BlockSpec((seq, 128), lambda b: (b, 0)),
                  pl.BlockSpec((1, CMP_LEN, 1, 128), lambda b: (layer, 0, 0, 0)),
                  pl.BlockSpec((1, CMP_LEN // 2, 256, 512), lambda b: (layer, 0, 0, 0)),
                  pl.BlockSpec((1, 512, 128), lambda b: (layer, 0, 0))],
        out_specs=pl.BlockSpec((None, n_tok, 128), lambda b: (b, 0, 0)),
        scratch_shapes=[pltpu.VMEM((seq + CMP_LEN, 128), F32)],
        compiler_params=_cparams("arbitrary"),
        name="compress_prompt",
    )(n4, posrows, w1p, w2p)


def _cmp_to_sel_weights(n_cmp, n_cmp_pad, n_sel_pad):
    ratio_sel = SEL_BLOCK // CMP_STRIDE
    ratio_cmp = CMP_LEN // CMP_STRIDE
    off = np.arange(n_cmp)[:, None] - ratio_sel * np.arange(n_sel_pad)[None, :]
    shifts = (np.arange(ratio_sel)[:, None] - np.arange(ratio_cmp)[None, :]).reshape(-1)
    w = np.sum(off[..., None] == shifts, axis=-1).astype(np.float32)
    return np.pad(w, ((0, n_cmp_pad - n_cmp), (0, 0)))


def _nsa_prompt_kernel(q_ref, kvc_ref, sel_ref, win_ref, ng_ref, wsel_ref, e_ref, o_ref,
                       kvc_b, sel_b, win_b, m_s, l_s, a_s, m_w, l_w, a_w, *, tq, tk, n_cmp):
    i = pl.program_id(1)

    @pl.when(i == 0)
    def _():
        kvc_b[...] = kvc_ref[...].astype(kvc_b.dtype)
        sel_b[...] = sel_ref[...].astype(sel_b.dtype)
        win_b[...] = win_ref[...].astype(win_b.dtype)

    rows = NSA_HEADS * tq
    q = q_ref[...]
    low = _lane((tq, 128)) < HEAD_DIM
    pieces = []
    for h in range(NSA_HEADS):
        blk = q[:, 128 * (h // 2):128 * (h // 2) + 128]
        if h % 2:
            blk = pltpu.roll(blk, HEAD_DIM, 1)
        pieces.append(jnp.where(low, blk, 0.0))
    qp = jnp.concatenate(pieces, axis=0).astype(MXU_DTYPE)
    q0 = i * tq
    pos = q0 + (lax.broadcasted_iota(jnp.int32, (rows, 1), 0) & (tq - 1))

    kvc = kvc_b[...]
    s = _dot_nt(qp, kvc)
    n = _lane(s.shape)
    vis = jnp.where(n * CMP_STRIDE + (CMP_LEN - 1) <= pos, n, n_cmp) < n_cmp
    s = jnp.where(vis, s, NEG)
    e = jnp.where(vis, jnp.exp(s - jnp.max(s, axis=-1, keepdims=True)), 0.0)
    den = jnp.sum(e, axis=-1, keepdims=True)
    p = (e / jnp.where(den > 0, den, 1.0)).astype(MXU_DTYPE)
    o_cmp = _dot(p, kvc)
    pw = _dot(p, wsel_ref[...])
    imp = pw[0:tq] + pw[tq:2 * tq] + pw[2 * tq:3 * tq] + pw[3 * tq:4 * tq]

    own = lax.shift_right_logical(q0 + lax.broadcasted_iota(jnp.int32, (tq, 1), 0), int(math.log2(SEL_BLOCK)))
    j = _lane(imp.shape)
    forced = jnp.where(j == 0, 1, jnp.where(j == own, 1, jnp.where(j == own - 1, 1, 0))) > 0
    impm = jnp.where(j <= own, jnp.where(forced, jnp.inf, imp), -jnp.inf)
    selm = jnp.where(j <= own, _top_k_mask(impm, SEL_TOPK, imp.shape[-1]), 0.0).astype(MXU_DTYPE)

    jd = q0 // tk

    def sel_scores(jb, diag):
        j0 = pl.multiple_of(jb * tk, tk)
        kv = sel_b[pl.ds(j0, tk), :]
        on = _dot(selm, e_ref[jb])
        on = jnp.concatenate([on] * NSA_HEADS, axis=0)
        if diag:
            key = j0 + _lane((rows, tk))
            on = jnp.where(key <= pos, on, 0.0)
        return jnp.where(on > 0.5, _dot_nt(qp, kv), NEG), kv

    _flash_first(*sel_scores(jd, True), m_s, l_s, a_s)

    def sel_body(jb, carry):
        _flash_next(*sel_scores(jb, False), m_s, l_s, a_s)
        return carry

    lax.fori_loop(0, jd, sel_body, 0)

    def win_scores(jb):
        j0 = pl.multiple_of(jb * tk, tk)
        kv = win_b[pl.ds(j0, tk), :]
        key = j0 + _lane((rows, tk))
        on = jnp.where(key <= pos, jnp.where(key > pos - WINDOW, 1, 0), 0) > 0
        return jnp.where(on, _dot_nt(qp, kv), NEG), kv

    _flash_first(*win_scores(jd), m_w, l_w, a_w)

    def win_body(jb, carry):
        _flash_next(*win_scores(jb), m_w, l_w, a_w)
        return carry

    lax.fori_loop(jnp.maximum(q0 - (WINDOW - 1), 0) // tk, jd, win_body, 0)

    n_pad = jnp.maximum(WINDOW - 1 - pos, 0).astype(F32)
    m_win = m_w[...]
    m_all = jnp.where(n_pad > 0, jnp.maximum(m_win, 0.0), m_win)
    shrink = jnp.exp(m_win - m_all)
    o_win = (a_w[...] * shrink) / (l_w[...] * shrink + n_pad * jnp.exp(-m_all))

    g = ng_ref[...]

    def gate(c):
        return jnp.concatenate([g[:, 3 * h + c:3 * h + c + 1] for h in range(NSA_HEADS)], axis=0)

    o = gate(0) * o_cmp + gate(1) * (a_s[...] / l_s[...]) + gate(2) * o_win
    o = jnp.where(_lane(o.shape) >= HEAD_DIM, o, 0.0)
    o_ref[...] = jnp.concatenate([o[h * tq:(h + 1) * tq] for h in range(NSA_HEADS)], axis=1).astype(o_ref.dtype)


def nsa_prompt(nq, kvc, n4, nwin, ng, *, batch, seq, tq=128, tk=256):
    nqb = seq // tq
    n_cmp = (seq - CMP_LEN) // CMP_STRIDE + 1
    n_cmp_pad = kvc.shape[1]
    wsel = jnp.asarray(_cmp_to_sel_weights(n_cmp, n_cmp_pad, 128), MXU_DTYPE)
    key_blk = (np.arange(seq) // SEL_BLOCK).reshape(seq // tk, 1, tk)
    expand = jnp.asarray(key_blk == np.arange(128).reshape(1, 128, 1), MXU_DTYPE)
    rows = NSA_HEADS * tq
    kern = functools.partial(_nsa_prompt_kernel, tq=tq, tk=tk, n_cmp=n_cmp)
    stat = pltpu.VMEM((rows, 1), F32)
    acc = pltpu.VMEM((rows, 128), F32)
    kv = pltpu.VMEM((seq, 128), MXU_DTYPE)
    return pl.pallas_call(
        kern,
        out_shape=jax.ShapeDtypeStruct((batch * seq, NSA_HEADS * 128), MXU_DTYPE),
        grid=(batch, nqb),
        in_specs=[pl.BlockSpec((tq, 256), lambda b, i: (b * nqb + i, 0)),
                  pl.BlockSpec((None, n_cmp_pad, 128), lambda b, i: (b, 0, 0)),
                  pl.BlockSpec((seq, 128), lambda b, i: (b, 1)),
                  pl.BlockSpec((seq, 128), lambda b, i: (b, 0)),
                  pl.BlockSpec((tq, 128), lambda b, i: (b * nqb + i, 0)),
                  pl.BlockSpec((n_cmp_pad, 128), lambda b, i: (0, 0)),
                  pl.BlockSpec((seq // tk, 128, tk), lambda b, i: (0, 0, 0))],
        out_specs=pl.BlockSpec((tq, NSA_HEADS * 128), lambda b, i: (b * nqb + i, 0)),
        scratch_shapes=[pltpu.VMEM((n_cmp_pad, 128), MXU_DTYPE), kv, kv, stat, stat, acc, stat, stat, acc],
        compiler_params=_cparams("arbitrary", "arbitrary"),
        name="nsa_prompt",
    )(nq, kvc, n4, nwin, ng, wsel, expand)


def _merge_kernel(x_ref, oa_ref, ob_ref, oc_ref, g_ref, sh_ref, sc_ref, gt_ref,
                  wbg_ref, wa_ref, wb_ref, wc_ref, wo_ref, o_ref):
    x = x_ref[...]
    h = _mod_norm(x, g_ref[0], sh_ref[...], sc_ref[...])
    bg = _sigmoid(_dot(h, wbg_ref[0]))
    d = D_MODEL
    m = (bg[:, 0:d] * _dot(oa_ref[...], wa_ref[0]) + bg[:, d:2 * d] * _dot(ob_ref[...], wb_ref[0])
         + bg[:, 2 * d:3 * d] * _dot(oc_ref[...], wc_ref[0]))
    o_ref[...] = x + gt_ref[...] * _dot(m, wo_ref[0])


def merge(x, oa, ob, oc, layer, g_mix, mod, wbg, wa, wb, wc, wo, *, tm, rows_per_batch, per_token):
    r = x.shape[0]
    nt = rows_per_batch // tm
    row = lambda w: pl.BlockSpec((tm, w), lambda i: (i, 0))
    wspec = lambda a: pl.BlockSpec((1,) + a.shape[1:], lambda i: (layer, 0, 0))
    return pl.pallas_call(
        _merge_kernel,
        out_shape=jax.ShapeDtypeStruct((r, D_MODEL), F32),
        grid=(r // tm,),
        in_specs=[row(D_MODEL), row(oa.shape[1]), row(ob.shape[1]), row(oc.shape[1]),
                  pl.BlockSpec((1, 1, D_MODEL), lambda i: (layer, 0, 0)),
                  _mod_spec(per_token, tm, nt, 0), _mod_spec(per_token, tm, nt, 1), _mod_spec(per_token, tm, nt, 2),
                  wspec(wbg), wspec(wa), wspec(wb), wspec(wc), wspec(wo)],
        out_specs=row(D_MODEL),
        compiler_params=_cparams("arbitrary"),
        name="merge",
    )(x, oa, ob, oc, g_mix, mod, mod, mod, wbg, wa, wb, wc, wo)


def _route(h, rw_ref, rb_ref):
    aff = _sigmoid(_dot(h, rw_ref[...]))
    biased = aff + rb_ref[...]
    lane = _lane(aff.shape)
    grp = lax.shift_right_logical(lane, int(math.log2(EXPERTS_PER_GROUP)))
    sentinel = jnp.int32(aff.shape[-1])
    best = None
    for gi in range(N_GROUPS):
        v1 = jnp.where(grp == gi, biased, -jnp.inf)
        m1 = jnp.max(v1, axis=-1, keepdims=True)
        i1 = jnp.min(jnp.where(v1 == m1, lane, sentinel), axis=-1, keepdims=True)
        v2 = jnp.where(lane == i1, -jnp.inf, v1)
        m2 = jnp.max(v2, axis=-1, keepdims=True)
        i2 = jnp.min(jnp.where(v2 == m2, lane, sentinel), axis=-1, keepdims=True)
        cand = (m1 + m2, i1, i2)
        if best is None:
            best = cand
        else:
            better = cand[0] > best[0]
            best = tuple(jnp.where(better, c, b) for c, b in zip(cand, best))
    _, i1, i2 = best
    w1 = jnp.sum(jnp.where(lane == i1, aff, 0.0), axis=-1, keepdims=True)
    w2 = jnp.sum(jnp.where(lane == i2, aff, 0.0), axis=-1, keepdims=True)
    tot = w1 + w2
    return jnp.where(lane == i1, w1 / tot, 0.0) + jnp.where(lane == i2, w2 / tot, 0.0)


def _moe_kernel(x_ref, g_ref, sh_ref, sc_ref, gt_ref, rw_ref, rb_ref, wg_ref, wu_ref, wd_ref, gf_ref,
                o_ref, hb, gate, acc, *, final):
    e = pl.program_id(1)

    @pl.when(e == 0)
    def _():
        h = _mod_norm(x_ref[...], g_ref[0], sh_ref[...], sc_ref[...])
        hb[...] = h.astype(hb.dtype)
        gate[...] = _route(h, rw_ref, rb_ref)
        acc[...] = jnp.zeros_like(acc)

    h = hb[...]
    gv = gate[...]
    ge = jnp.sum(jnp.where(_lane(gv.shape) == e, gv, 0.0), axis=-1, keepdims=True)
    hid = _silu(_dot(h, wg_ref[0, 0])) * _dot(h, wu_ref[0, 0]) * ge
    acc[...] += _dot(hid, wd_ref[0, 0])

    @pl.when(e == N_EXPERTS - 1)
    def _():
        y = x_ref[...] + gt_ref[...] * acc[...]
        if final:
            y = _rms(y) * gf_ref[...]
        o_ref[...] = y


def moe(x, layer, g_ffn, mod, rw, rb, wg, wu, wd, g_final, *, tm, rows_per_batch, per_token, final):
    r = x.shape[0]
    nt = rows_per_batch // tm
    row = pl.BlockSpec((tm, D_MODEL), lambda i, e: (i, 0))
    if per_token:
        mspec = lambda c: pl.BlockSpec((tm, D_MODEL), lambda i, e: (i, c))
    else:
        mspec = lambda c: pl.BlockSpec((None, 1, D_MODEL), lambda i, e: (i // nt, 0, c))
    kern = functools.partial(_moe_kernel, final=final)
    return pl.pallas_call(
        kern,
        out_shape=jax.ShapeDtypeStruct((r, D_MODEL), F32),
        grid=(r // tm, N_EXPERTS),
        in_specs=[row, pl.BlockSpec((1, 1, D_MODEL), lambda i, e: (layer, 0, 0)),
                  mspec(3), mspec(4), mspec(5),
                  pl.BlockSpec((D_MODEL, 128), lambda i, e: (0, 0)),
                  pl.BlockSpec((1, 128), lambda i, e: (0, 0)),
                  pl.BlockSpec((1, 1, D_MODEL, EXPERT_FF), lambda i, e: (layer, e, 0, 0)),
                  pl.BlockSpec((1, 1, D_MODEL, EXPERT_FF), lambda i, e: (layer, e, 0, 0)),
                  pl.BlockSpec((1, 1, EXPERT_FF, D_MODEL), lambda i, e: (layer, e, 0, 0)),
                  pl.BlockSpec((1, D_MODEL), lambda i, e: (0, 0))],
        out_specs=row,
        scratch_shapes=[pltpu.VMEM((tm, D_MODEL), MXU_DTYPE), pltpu.VMEM((tm, 128), F32),
                        pltpu.VMEM((tm, D_MODEL), F32)],
        compiler_params=_cparams("arbitrary", "arbitrary"),
        name="moe",
    )(x, g_ffn, mod, mod, mod, rw, rb, wg, wu, wd, g_final)


def _page_specs(n_pages, pages_per_step, layer, width):
    def spec(p):
        return pl.BlockSpec((None, None, PAGE_SIZE, width),
                            lambda b, j, pt: (layer, pt[b * n_pages + j * pages_per_step + p], 0, 0))
    return [spec(p) for p in range(pages_per_step)]


def _new_token_mask(rows, n_new):
    t = lax.broadcasted_iota(jnp.int32, (rows, 8), 0) & (n_new - 1)
    tk = _lane((rows, 8))
    return jnp.where(tk <= t, jnp.where(tk < n_new, 1, 0), 0) > 0


def _fold_heads(o):
    a = o[0:8] + o[8:16]
    return a + pltpu.roll(a, 4, 0)


def _diff_sample_kernel(pt_ref, q_ref, *rest, n_pg, lam_init):
    pages = rest[:n_pg]
    new_ref, lam_ref, g_ref, o_ref, m_ref, l_ref, a_ref = rest[n_pg:]
    j = pl.program_id(1)

    @pl.when(j == 0)
    def _():
        m_ref[...] = jnp.full(m_ref.shape, NEG, F32)
        l_ref[...] = jnp.zeros(l_ref.shape, F32)
        a_ref[...] = jnp.zeros(a_ref.shape, F32)

    q = q_ref[...]
    for pg in pages:
        page = pg[...]
        _flash_next(_dot_nt(q, page[:, 0:512]), page[:, 512:1024], m_ref, l_ref, a_ref)

    @pl.when(j == pl.num_programs(1) - 1)
    def _():
        new = new_ref[...]
        s = jnp.where(_new_token_mask(32, 4), _dot_nt(q, new[:, 0:512]), NEG)
        _flash_next(s, new[:, 512:1024], m_ref, l_ref, a_ref)
        a = a_ref[...] / l_ref[...]
        o = a[0:16] - _diff_lambda(lam_ref, lam_init) * a[16:32]
        own = lax.shift_right_logical(_lane(o.shape), 7) == lax.shift_right_logical(_lane(o.shape, 0), 2)
        o = jnp.where(own, o, 0.0)
        o = o * lax.rsqrt(jnp.sum(o * o, axis=-1, keepdims=True) * (1.0 / DIFF_VDIM) + RMS_EPS)
        g = jnp.concatenate([g_ref[0]] * DIFF_HEADS, axis=1)
        o_ref[...] = _fold_heads(o * g * (1.0 - lam_init)).astype(o_ref.dtype)


def diff_sample(page_table, qrows, cache, new_kv, diff_lambda, diff_norm_g, layer, *, pages_per_step=16):
    n_seq, n_pages = page_table.shape
    lam_init = 0.8 - 0.6 * math.exp(-0.3 * layer)
    kern = functools.partial(_diff_sample_kernel, n_pg=pages_per_step, lam_init=lam_init)
    grid_spec = pltpu.PrefetchScalarGridSpec(
        num_scalar_prefetch=1,
        grid=(n_seq, n_pages // pages_per_step),
        in_specs=[pl.BlockSpec((None, 32, 512), lambda b, j, pt: (b, 0, 0))]
        + _page_specs(n_pages, pages_per_step, layer, 1024)
        + [pl.BlockSpec((None, 8, 1024), lambda b, j, pt: (b, 0, 0)),
           pl.BlockSpec((1, 4, HEAD_DIM), lambda b, j, pt: (layer, 0, 0)),
           pl.BlockSpec((1, 1, DIFF_VDIM), lambda b, j, pt: (layer, 0, 0))],
        out_specs=pl.BlockSpec((None, 8, 512), lambda b, j, pt: (b, 0, 0)),
        scratch_shapes=[pltpu.VMEM((32, 1), F32), pltpu.VMEM((32, 1), F32), pltpu.VMEM((32, 512), F32)],
    )
    return pl.pallas_call(
        kern,
        out_shape=jax.ShapeDtypeStruct((n_seq, 8, 512), MXU_DTYPE),
        grid_spec=grid_spec,
        compiler_params=_cparams("arbitrary", "arbitrary"),
        name="diff_sample",
    )(page_table.reshape(-1), qrows, *([cache] * pages_per_step), new_kv, diff_lambda, diff_norm_g)


def _moba_sample_kernel(pt_ref, q_ref, *rest, n_pg, n_blk):
    pages = rest[:n_pg]
    new_ref, o_ref, m_all, l_all, acc_all, kmean = rest[n_pg:]
    j = pl.program_id(1)

    @pl.when(j == 0)
    def _():
        m_all[...] = jnp.full(m_all.shape, NEG, F32)
        l_all[...] = jnp.zeros(l_all.shape, F32)
        kmean[...] = jnp.zeros(kmean.shape, F32)

    q = q_ref[...]
    lane = _lane(m_all.shape)
    krow = _lane(kmean.shape, 0)
    for b2 in range(n_pg // 2):
        n = j * (n_pg // 2) + b2
        pa, pb = pages[2 * b2][...], pages[2 * b2 + 1][...]
        k = jnp.concatenate([pa[:, 0:256], pb[:, 0:256]], axis=0)
        v = jnp.concatenate([pa[:, 256:512], pb[:, 256:512]], axis=0)
        s = _dot_nt(q, k)
        m = jnp.max(s, axis=-1, keepdims=True)
        p = jnp.exp(s - m)
        m_all[...] = jnp.where(lane == n, m, m_all[...])
        l_all[...] = jnp.where(lane == n, jnp.sum(p, axis=-1, keepdims=True), l_all[...])
        acc_all[n] = _dot(p, v)
        kmean[...] = jnp.where(krow == n, jnp.sum(k, axis=0, keepdims=True) * (1.0 / MOBA_BLOCK), kmean[...])

    @pl.when(j == pl.num_programs(1) - 1)
    def _():
        gate = _dot_nt(q, kmean[...])
        sel = jnp.where(lane < n_blk, _top_k_mask(gate, MOBA_TOPK, n_blk), 0.0) > 0.5
        new = new_ref[...]
        s_new = jnp.where(_new_token_mask(16, 4), _dot_nt(q, new[:, 0:256]), NEG)
        m_new = jnp.max(s_new, axis=-1, keepdims=True)
        p_new = jnp.exp(s_new - m_new)
        m_tot = jnp.maximum(m_new, jnp.max(jnp.where(sel, m_all[...], NEG), axis=-1, keepdims=True))
        w = jnp.where(sel, jnp.exp(m_all[...] - m_tot), 0.0)
        w_new = jnp.exp(m_new - m_tot)
        l_tot = jnp.sum(w * l_all[...], axis=-1, keepdims=True) + w_new * jnp.sum(p_new, axis=-1, keepdims=True)

        def body(n, acc):
            wn = jnp.sum(jnp.where(lane == n, w, 0.0), axis=-1, keepdims=True)
            return acc + wn * acc_all[n]

        acc = lax.fori_loop(0, n_blk, body, w_new * _dot(p_new, new[:, 256:512]))
        o = acc / l_tot
        own = lax.shift_right_logical(_lane(o.shape), 6) == lax.shift_right_logical(_lane(o.shape, 0), 2)
        o_ref[...] = _fold_heads(jnp.where(own, o, 0.0)).astype(o_ref.dtype)


def moba_sample(page_table, qrows, cache, new_kv, layer, *, pages_per_step=16):
    n_seq, n_pages = page_table.shape
    n_blk = n_pages * PAGE_SIZE // MOBA_BLOCK
    kern = functools.partial(_moba_sample_kernel, n_pg=pages_per_step, n_blk=n_blk)
    grid_spec = pltpu.PrefetchScalarGridSpec(
        num_scalar_prefetch=1,
        grid=(n_seq, n_pages // pages_per_step),
        in_specs=[pl.BlockSpec((None, 16, 256), lambda b, j, pt: (b, 0, 0))]
        + _page_specs(n_pages, pages_per_step, layer, 512)
        + [pl.BlockSpec((None, 8, 512), lambda b, j, pt: (b, 0, 0))],
        out_specs=pl.BlockSpec((None, 8, 256), lambda b, j, pt: (b, 0, 0)),
        scratch_shapes=[pltpu.VMEM((16, 128), F32), pltpu.VMEM((16, 128), F32),
                        pltpu.VMEM((n_blk, 16, 256), F32), pltpu.VMEM((128, 256), F32)],
    )
    return pl.pallas_call(
        kern,
        out_shape=jax.ShapeDtypeStruct((n_seq, 8, 256), MXU_DTYPE),
        grid_spec=grid_spec,
        compiler_params=_cparams("arbitrary", "arbitrary"),
        name="moba_sample",
    )(page_table.reshape(-1), qrows, *([cache] * pages_per_step), new_kv)


def _nsa_sample_kernel(pt_ref, qa_ref, qs_ref, *rest, n_pg, past, n_cmp):
    pages = rest[:n_pg]
    (win_ref, new_ref, ng_ref, pos_ref, w1_ref, w2_ref, wsel_ref, o_ref, nwin_ref,
     xs, m_all, l_all, acc_all) = rest[n_pg:]
    j = pl.program_id(1)
    n_selp = m_all.shape[1]

    @pl.when(j == 0)
    def _():
        xs[past:past + CMP_LEN, :] = jnp.zeros((CMP_LEN, 128), F32)
        m_all[...] = jnp.full(m_all.shape, NEG, F32)
        l_all[...] = jnp.zeros(l_all.shape, F32)

    qa = qa_ref[...]
    qs = qs_ref[...]
    lane_b = _lane(m_all.shape)
    low = _lane((16, PAGE_SIZE)) < SEL_BLOCK
    for pi, pg in enumerate(pages):
        page = pg[...]
        row0 = pl.multiple_of((j * n_pg + pi) * PAGE_SIZE, PAGE_SIZE)
        xs[pl.ds(row0, PAGE_SIZE), :] = page[:, 0:128]
        s = _dot_nt(qs, page)
        m_lo = jnp.max(jnp.where(low, s, NEG), axis=-1, keepdims=True)
        m_hi = jnp.max(jnp.where(low, NEG, s), axis=-1, keepdims=True)
        p = jnp.exp(s - jnp.where(low, m_lo, m_hi))
        p_lo, p_hi = jnp.where(low, p, 0.0), jnp.where(low, 0.0, p)
        n0 = 2 * (j * n_pg + pi)
        m_all[...] = jnp.where(lane_b == n0, m_lo, jnp.where(lane_b == n0 + 1, m_hi, m_all[...]))
        l_all[...] = jnp.where(lane_b == n0, jnp.sum(p_lo, axis=-1, keepdims=True),
                               jnp.where(lane_b == n0 + 1, jnp.sum(p_hi, axis=-1, keepdims=True), l_all[...]))
        acc_all[n0] = _dot(p_lo, page[:, 128:256])
        acc_all[n0 + 1] = _dot(p_hi, page[:, 128:256])

    @pl.when(j == pl.num_programs(1) - 1)
    def _():
        new = new_ref[...]
        t16 = lax.broadcasted_iota(jnp.int32, (16, 1), 0) & 3
        new_ok = _new_token_mask(16, 4)

        kvc = _compress_rows(xs, n_cmp + 1, pos_ref, w1_ref, w2_ref).astype(MXU_DTYPE)
        s = _dot_nt(qa, kvc)
        n = _lane(s.shape)
        vis = jnp.where(n * CMP_STRIDE + (CMP_LEN - 1) <= past + t16, n, n_cmp) < n_cmp
        s = jnp.where(vis, s, NEG)
        e = jnp.where(vis, jnp.exp(s - jnp.max(s, axis=-1, keepdims=True)), 0.0)
        den = jnp.sum(e, axis=-1, keepdims=True)
        p = (e / jnp.where(den > 0, den, 1.0)).astype(MXU_DTYPE)
        o_cmp = _dot(p, kvc)
        pw = _dot(p, wsel_ref[...])
        imp = pw[0:8] + pw[8:16]
        imp = imp + pltpu.roll(imp, 4, 0)

        own = lax.shift_right_logical(past + (lax.broadcasted_iota(jnp.int32, (8, 1), 0) & 3), int(math.log2(SEL_BLOCK)))
        jb = _lane(imp.shape)
        forced = jnp.where(jb == 0, 1, jnp.where(jb == own, 1, jnp.where(jb == own - 1, 1, 0))) > 0
        impm = jnp.where(jb <= own, jnp.where(forced, jnp.inf, imp), -jnp.inf)
        selm = jnp.where(jb <= own, _top_k_mask(impm, SEL_TOPK, imp.shape[-1]), 0.0)
        sel = jnp.concatenate([selm, selm], axis=0)[:, 0:n_selp] > 0.5

        s_new = jnp.where(new_ok, _dot_nt(qs, new[:, 0:256]), NEG)
        m_new = jnp.max(s_new, axis=-1, keepdims=True)
        p_new = jnp.exp(s_new - m_new)
        m_tot = jnp.maximum(m_new, jnp.max(jnp.where(sel, m_all[...], NEG), axis=-1, keepdims=True))
        w = jnp.where(sel, jnp.exp(m_all[...] - m_tot), 0.0)
        w_new = jnp.exp(m_new - m_tot)
        l_tot = jnp.sum(w * l_all[...], axis=-1, keepdims=True) + w_new * jnp.sum(p_new, axis=-1, keepdims=True)

        def body(nb, acc):
            wn = jnp.sum(jnp.where(lane_b == nb, w, 0.0), axis=-1, keepdims=True)
            return acc + wn * acc_all[nb]

        o_sel = lax.fori_loop(0, n_selp, body, w_new * _dot(p_new, new[:, 128:256])) / l_tot

        win = win_ref[...]
        keep = win.shape[0]
        s_w = _dot_nt(qa, win)
        s_w = jnp.where(_lane(s_w.shape) > t16 + (keep - WINDOW), s_w, NEG)
        s_wn = jnp.where(new_ok, _dot_nt(qa, new[:, 256:384]), NEG)
        m_w = jnp.maximum(jnp.max(s_w, axis=-1, keepdims=True), jnp.max(s_wn, axis=-1, keepdims=True))
        p_w, p_wn = jnp.exp(s_w - m_w), jnp.exp(s_wn - m_w)
        l_w = jnp.sum(p_w, axis=-1, keepdims=True) + jnp.sum(p_wn, axis=-1, keepdims=True)
        o_win = (_dot(p_w, win) + _dot(p_wn, new[:, 256:384])) / l_w

        g = ng_ref[...]
        top_rows = lax.broadcasted_iota(jnp.int32, (8, 1), 0) < 4

        def gate(c):
            col = lambda h: g[:, 3 * h + c:3 * h + c + 1]
            return jnp.concatenate([jnp.where(top_rows, col(0), col(1)), jnp.where(top_rows, col(2), col(3))], axis=0)

        o = gate(0) * o_cmp + gate(1) * o_sel + gate(2) * o_win
        o = jnp.where(_lane(o.shape) >= HEAD_DIM, o, 0.0)
        top, bot = o[0:8], o[8:16]
        o_ref[...] = jnp.concatenate([top, pltpu.roll(top, 4, 0), bot, pltpu.roll(bot, 4, 0)], axis=1).astype(o_ref.dtype)

        nwin_ref[0:keep - 4, :] = win_ref[4:keep, :]
        nwin_ref[keep - 4:keep, :] = new[0:4, 256:384]


def nsa_sample(page_table, qa, qs, cache, win_state, new_rows, ng8, posrows, w1p, w2p, layer, *, pages_per_step=16):
    n_seq, n_pages = page_table.shape
    past = n_pages * PAGE_SIZE
    keep = win_state.shape[2]
    total = past + 4
    n_cmp = (total - CMP_LEN) // CMP_STRIDE + 1
    n_selp = past // SEL_BLOCK
    n_sel_pad = -(-(n_selp + 1) // LANES) * LANES
    wsel = jnp.asarray(_cmp_to_sel_weights(n_cmp, n_cmp + 1, n_sel_pad), MXU_DTYPE)
    kern = functools.partial(_nsa_sample_kernel, n_pg=pages_per_step, past=past, n_cmp=n_cmp)
    const = lambda shape: pl.BlockSpec(shape, lambda b, j, pt: (0,) * len(shape))
    grid_spec = pltpu.PrefetchScalarGridSpec(
        num_scalar_prefetch=1,
        grid=(n_seq, n_pages // pages_per_step),
        in_specs=[pl.BlockSpec((None, 16, 128), lambda b, j, pt: (b, 0, 0)),
                  pl.BlockSpec((None, 16, 256), lambda b, j, pt: (b, 0, 0))]
        + _page_specs(n_pages, pages_per_step, layer, 256)
        + [pl.BlockSpec((None, None, keep, 128), lambda b, j, pt: (layer, b, 0, 0)),
           pl.BlockSpec((None, 8, 384), lambda b, j, pt: (b, 0, 0)),
           pl.BlockSpec((None, 8, 128), lambda b, j, pt: (b, 0, 0)),
           pl.BlockSpec((1, CMP_LEN, 1, 128), lambda b, j, pt: (layer, 0, 0, 0)),
           pl.BlockSpec((1, CMP_LEN // 2, 256, 512), lambda b, j, pt: (layer, 0, 0, 0)),
           pl.BlockSpec((1, 512, 128), lambda b, j, pt: (layer, 0, 0)),
           const((n_cmp + 1, n_sel_pad))],
        out_specs=(pl.BlockSpec((None, 8, 512), lambda b, j, pt: (b, 0, 0)),
                   pl.BlockSpec((None, keep, 128), lambda b, j, pt: (b, 0, 0))),
        scratch_shapes=[pltpu.VMEM((past + CMP_LEN, 128), F32), pltpu.VMEM((16, n_selp), F32),
                        pltpu.VMEM((16, n_selp), F32), pltpu.VMEM((n_selp, 16, 128), F32)],
    )
    return pl.pallas_call(
        kern,
        out_shape=(jax.ShapeDtypeStruct((n_seq, 8, 512), MXU_DTYPE), jax.ShapeDtypeStruct((n_seq, keep, 128), F32)),
        grid_spec=grid_spec,
        compiler_params=_cparams("arbitrary", "arbitrary"),
        name="nsa_sample",
    )(page_table.reshape(-1), qa, qs, *([cache] * pages_per_step), win_state, new_rows, ng8, posrows, w1p, w2p, wsel)


def sample_operands(dq, dkv, mq, mkv, nq, n4, nwin, ng, *, n_seq):
    per_seq = lambda a: a.astype(F32).reshape(n_seq, N_NEW, a.shape[-1])
    pad8 = lambda a: jnp.pad(a, ((0, 0), (0, 8 - N_NEW), (0, 0)))
    col = np.arange(512)
    head_ok = (col // 128)[None, None, :] == np.arange(DIFF_HEADS)[None, :, None]
    map_ok = ((col % 128) // HEAD_DIM)[None, None, :] == np.arange(2)[:, None, None]
    dmask = jnp.asarray(head_ok & map_ok, F32)
    diff_q = (per_seq(dq)[:, None, None, :, :] * dmask[None, :, :, None, :]).reshape(n_seq, 32, 512)
    mmask = jnp.asarray((np.arange(256) // HEAD_DIM)[None, :] == np.arange(MOBA_HEADS)[:, None], F32)
    moba_q = (per_seq(mq)[:, None, :, :] * mmask[None, :, None, :]).reshape(n_seq, 16, 256)
    q = per_seq(nq).reshape(n_seq, N_NEW, NSA_HEADS, HEAD_DIM).transpose(0, 2, 1, 3).reshape(n_seq, 16, HEAD_DIM)
    z = jnp.zeros_like(q)
    g = per_seq(ng)
    return {
        "diff_q": diff_q.astype(MXU_DTYPE), "diff_new": pad8(per_seq(dkv)),
        "moba_q": moba_q.astype(MXU_DTYPE), "moba_new": pad8(per_seq(mkv)),
        "nsa_qa": jnp.concatenate([q, z], axis=-1).astype(MXU_DTYPE),
        "nsa_qs": jnp.concatenate([z, z, q, z], axis=-1).astype(MXU_DTYPE),
        "nsa_new": pad8(jnp.concatenate([per_seq(n4), per_seq(nwin)], axis=-1)),
        "nsa_g": jnp.concatenate([g, g], axis=1),
    }


def kernel(x_prompt, x_sample, cache_diff_kv, cache_moba_kv, cache_nsa_kv, state_nsa_win, page_table, c_prompt, c_sample, w_ada, b_ada, g_mix, g_ffn, w_in, diff_lambda, diff_norm_g, cmp_pos, cmp_w1, cmp_w2, w_br_a, w_br_b, w_br_c, w_out, router_w, router_b, w_e_gate, w_e_up, w_e_down, g_final):
    batch, seq, d = x_prompt.shape
    n_seq, n_new, _ = x_sample.shape
    assert n_new == N_NEW and d == D_MODEL and seq % MOBA_BLOCK == 0
    depth = w_ada.shape[0]
    n_phys = cache_diff_kv.shape[1]
    past = page_table.shape[1] * PAGE_SIZE
    keep = state_nsa_win.shape[2]
    rows_s = n_seq * n_new

    n_cond = batch + n_seq
    c_all = jnp.concatenate([c_prompt, c_sample, jnp.zeros((-n_cond % 8, d), F32)], axis=0)
    mod = ada_ln(c_all, w_ada, b_ada)
    w1 = jnp.pad(w_in[:, :, :PROJ_COLS], ((0, 0), (0, 0), (0, PROJ_PAD - PROJ_COLS))).astype(MXU_DTYPE)
    wbg = w_in[:, :, PROJ_COLS:].astype(MXU_DTYPE)
    wa, wb, wo = w_br_a.astype(MXU_DTYPE), w_br_b.astype(MXU_DTYPE), w_out.astype(MXU_DTYPE)
    wc4 = w_br_c.reshape(depth, NSA_HEADS, HEAD_DIM, d)
    wc = jnp.concatenate([jnp.zeros_like(wc4), wc4], axis=2).reshape(depth, NSA_HEADS * 128, d).astype(MXU_DTYPE)
    rw = jnp.pad(router_w, ((0, 0), (0, LANES - N_EXPERTS))).astype(MXU_DTYPE)
    rb = jnp.pad(router_b, (0, LANES - N_EXPERTS)).reshape(1, LANES)
    wg, wu, wd = w_e_gate.astype(MXU_DTYPE), w_e_up.astype(MXU_DTYPE), w_e_down.astype(MXU_DTYPE)
    posrows, w1p, w2p = compress_weights(cmp_pos, cmp_w1, cmp_w2)
    g_mix3, g_ffn3 = g_mix.reshape(depth, 1, d), g_ffn.reshape(depth, 1, d)
    gf = g_final.reshape(1, d)
    norm_g3 = diff_norm_g.reshape(depth, 1, DIFF_VDIM)
    tables_p = rope_tables(jnp.arange(seq, dtype=jnp.int32))
    tables_s = rope_tables(jnp.tile(past + jnp.arange(n_new, dtype=jnp.int32), n_seq))
    cache_d = cache_diff_kv.reshape(depth, n_phys, PAGE_SIZE, 2 * DIFF_HEADS * DIFF_VDIM)
    cache_m = cache_moba_kv.reshape(depth, n_phys, PAGE_SIZE, 2 * MOBA_HEADS * HEAD_DIM)
    cache_n = cache_nsa_kv.reshape(depth, n_phys, PAGE_SIZE, 4 * HEAD_DIM)
    win_state = state_nsa_win.reshape(depth, n_seq, keep, 2 * HEAD_DIM)

    xp = x_prompt.reshape(batch * seq, d)
    xs = x_sample.reshape(rows_s, d)
    prompt = dict(rows_per_batch=seq, per_token=False)
    sample = dict(tm=rows_s, rows_per_batch=rows_s, per_token=True)
    leaves = [[] for _ in range(8)]
    for l in range(depth):
        last = l == depth - 1
        modp = mod[l, :batch].reshape(batch, 1, 6 * d)
        dq, dkv, mq, mkv, nq, n4, nwin, ng = project(xp, l, g_mix3, modp, modp, w1, tables_p, tm=256, **prompt)
        oa = diff_prompt(dq, dkv, diff_lambda, norm_g3, l, batch=batch, seq=seq)
        ob = moba_prompt(mq, mkv, batch=batch, seq=seq)
        kvc = compress_prompt(n4, posrows, w1p, w2p, l, batch=batch, seq=seq)
        oc = nsa_prompt(nq, kvc, n4, nwin, ng, batch=batch, seq=seq)
        xp = merge(xp, oa, ob, oc, l, g_mix3, modp, wbg, wa, wb, wc, wo, tm=512, **prompt)
        xp = moe(xp, l, g_ffn3, modp, rw, rb, wg, wu, wd, gf, tm=1024, final=last, **prompt)
        leaves[0].append(dkv.reshape(batch, seq, 2, DIFF_HEADS, DIFF_VDIM))
        leaves[2].append(mkv.reshape(batch, seq, 2, MOBA_HEADS, HEAD_DIM))
        leaves[4].append(n4.reshape(batch, seq, 4, 1, HEAD_DIM))
        leaves[6].append(nwin.reshape(batch, seq, 2, 1, HEAD_DIM)[:, seq - min(WINDOW, seq):])
        mods = jnp.repeat(mod[l, batch:n_cond], n_new, axis=0)
        dq, dkv, mq, mkv, nq, n4, nwin, ng = project(xs, l, g_mix3, mods, mods, w1, tables_s, **sample)
        ops = sample_operands(dq, dkv, mq, mkv, nq, n4, nwin, ng, n_seq=n_seq)
        oa = diff_sample(page_table, ops["diff_q"], cache_d, ops["diff_new"], diff_lambda, norm_g3, l)
        ob = moba_sample(page_table, ops["moba_q"], cache_m, ops["moba_new"], l)
        oc, new_win = nsa_sample(page_table, ops["nsa_qa"], ops["nsa_qs"], cache_n, win_state, ops["nsa_new"],
                                 ops["nsa_g"], posrows, w1p, w2p, l)
        tok = lambda a: a[:, :n_new].reshape(rows_s, a.shape[-1])
        xs = merge(xs, tok(oa), tok(ob), tok(oc), l, g_mix3, mods, wbg, wa, wb, wc, wo, **sample)
        xs = moe(xs, l, g_ffn3, mods, rw, rb, wg, wu, wd, gf, final=last, **sample)
        leaves[1].append(dkv.reshape(n_seq, n_new, 2, DIFF_HEADS, DIFF_VDIM))
        leaves[3].append(mkv.reshape(n_seq, n_new, 2, MOBA_HEADS, HEAD_DIM))
        leaves[5].append(n4.reshape(n_seq, n_new, 4, 1, HEAD_DIM))
        leaves[7].append(new_win.reshape(n_seq, keep, 2, 1, HEAD_DIM))
    return (xp.reshape(batch, seq, d), xs.reshape(n_seq, n_new, d)) + tuple(jnp.stack(v) for v in leaves)
```

```python
import functools
import math

import jax
import jax.numpy as jnp
import numpy as np
from jax import lax
from jax.experimental import pallas as pl
from jax.experimental.pallas import tpu as pltpu

D_MODEL = 1024
HEAD_DIM = 64
HALF = HEAD_DIM // 2
ROPE_THETA = 10000.0
RMS_EPS = 1e-6
PAGE_SIZE = 128
DIFF_HEADS = 4
DIFF_VDIM = 2 * HEAD_DIM
MOBA_HEADS = 4
MOBA_BLOCK = 256
MOBA_TOPK = 3
NSA_HEADS = 4
CMP_LEN = 32
CMP_STRIDE = 16
CMP_HIDDEN = 256
SEL_BLOCK = 64
SEL_TOPK = 16
WINDOW = 512
N_BRANCH = 3
N_NEW = 4
N_EXPERTS = 16
N_GROUPS = 4
EXPERTS_PER_GROUP = N_EXPERTS // N_GROUPS
EXPERT_FF = 512
PROJ_COLS = 2956
PROJ_PAD = 3072
Q_SCALE = HEAD_DIM ** -0.5

LANES = 128
VMEM_LIMIT_BYTES = 56 * 1024 * 1024

MXU_DTYPE = jnp.bfloat16
F32 = jnp.float32
NEG = -1e30


def _cparams(*sem):
    return pltpu.CompilerParams(dimension_semantics=sem, vmem_limit_bytes=VMEM_LIMIT_BYTES)


def _sigmoid(x):
    return 1.0 / (1.0 + jnp.exp(-x))


def _silu(x):
    return x * _sigmoid(x)


def _gelu_tanh(x):
    c = math.sqrt(2.0 / math.pi)
    return 0.5 * x * (1.0 + jnp.tanh(c * (x + 0.044715 * (x * x * x))))


def _dot(a, b):
    return jnp.dot(a.astype(MXU_DTYPE), b.astype(MXU_DTYPE), preferred_element_type=F32)


def _dot_nt(a, b):
    return lax.dot_general(a.astype(MXU_DTYPE), b.astype(MXU_DTYPE), (((1,), (1,)), ((), ())),
                           preferred_element_type=F32)


def _rms(x):
    return x * lax.rsqrt(jnp.mean(x * x, axis=-1, keepdims=True) + RMS_EPS)


def _mod_norm(x, g, shift, scale):
    return (_rms(x) * g) * (1.0 + scale) + shift


def _lane(shape, dim=None):
    return lax.broadcasted_iota(jnp.int32, shape, len(shape) - 1 if dim is None else dim)


def _rope(y, cos_t, sin_t):
    outs = []
    for c in range(y.shape[1] // LANES):
        blk = y[:, c * LANES:(c + 1) * LANES]
        first = (_lane(blk.shape) % HEAD_DIM) < HALF
        partner = jnp.where(first, pltpu.roll(blk, LANES - HALF, 1), pltpu.roll(blk, HALF, 1))
        outs.append(blk * cos_t + partner * sin_t)
    return outs[0] if len(outs) == 1 else jnp.concatenate(outs, axis=1)


def _flash_next(s, v_fn, m_ref, l_ref, acc_ref):
    m_old = m_ref[...]
    m = jnp.maximum(m_old, jnp.max(s, axis=-1, keepdims=True))
    alpha = jnp.exp(m_old - m)
    p = jnp.exp(s - m)
    m_ref[...] = m
    l_ref[...] = alpha * l_ref[...] + jnp.sum(p, axis=-1, keepdims=True)
    acc_ref[...] = alpha * acc_ref[...] + v_fn(p)


def _flash_t_next(s, vt, m_ref, l_ref, acc_ref):
    m_old = m_ref[...]
    m = jnp.maximum(m_old, jnp.max(s, axis=0, keepdims=True))
    alpha = jnp.exp(m_old - m)
    p = jnp.exp(s - m)
    m_ref[...] = m
    l_ref[...] = alpha * l_ref[...] + jnp.sum(p, axis=0, keepdims=True)
    acc_ref[...] = alpha * acc_ref[...] + _dot(vt, p)


def _flash_t_loop(score_fn, vt_ref, stats, sbuf, first_blk, lo, hi):
    for m_ref, l_ref, acc_ref in stats:
        m_ref[...] = jnp.full(m_ref.shape, NEG, F32)
        l_ref[...] = jnp.zeros(l_ref.shape, F32)
        acc_ref[...] = jnp.zeros(acc_ref.shape, F32)

    if sbuf is None:
        def direct(blk, first):
            vt = vt_ref[blk]
            for s, (m_ref, l_ref, acc_ref) in zip(score_fn(blk, first), stats):
                _flash_t_next(s, vt, m_ref, l_ref, acc_ref)

        direct(first_blk, True)

        def body(t, carry):
            direct(t, False)
            return carry

        lax.fori_loop(lo, hi, body, 0)
        return

    for mi, s in enumerate(score_fn(first_blk, True)):
        sbuf[0, mi] = s

    def consume(slot, blk):
        vt = vt_ref[blk]
        for mi, (m_ref, l_ref, acc_ref) in enumerate(stats):
            _flash_t_next(sbuf[slot, mi], vt, m_ref, l_ref, acc_ref)

    def body(t, carry):
        k = t - lo
        nxt = score_fn(t, False)
        consume(k & 1, jnp.where(k == 0, first_blk, t - 1))
        for mi, s in enumerate(nxt):
            sbuf[1 - (k & 1), mi] = s
        return carry

    lax.fori_loop(lo, hi, body, 0)
    n = hi - lo
    consume(n & 1, jnp.where(n == 0, first_blk, hi - 1))


def _top_k_rows(vals, k):
    row = _lane(vals.shape, 0)
    big = jnp.int32(vals.shape[0])

    def body(_, carry):
        sel, cur = carry
        m = jnp.max(cur, axis=0, keepdims=True)
        idx = jnp.min(jnp.where(cur == m, row, big), axis=0, keepdims=True)
        hit = row == idx
        return jnp.where(hit, 1.0, sel), jnp.where(hit, -jnp.inf, cur)

    sel, _ = lax.fori_loop(0, k, body, (jnp.zeros(vals.shape, F32), vals))
    return sel


def _top_k_mask(vals, k, n_valid_lanes):
    lane = _lane(vals.shape)
    big = jnp.int32(vals.shape[-1])

    def body(_, carry):
        sel, cur = carry
        m = jnp.max(cur, axis=-1, keepdims=True)
        idx = jnp.min(jnp.where(cur == m, lane, big), axis=-1, keepdims=True)
        hit = lane == idx
        return jnp.where(hit, 1.0, sel), jnp.where(hit, -jnp.inf, cur)

    cur0 = jnp.where(lane < n_valid_lanes, vals, -jnp.inf)
    sel, _ = lax.fori_loop(0, k, body, (jnp.zeros(vals.shape, F32), cur0))
    return sel


def _ada_kernel(c_ref, w_ref, b_ref, o_ref):
    o_ref[0] = _dot(_silu(c_ref[...]), w_ref[0]) + b_ref[0]


def ada_ln(c_all, w_ada, b_ada):
    depth, _, n = w_ada.shape
    rc = c_all.shape[0]
    tn = 1536
    return pl.pallas_call(
        _ada_kernel,
        out_shape=jax.ShapeDtypeStruct((depth, rc, n), F32),
        grid=(depth, n // tn),
        in_specs=[pl.BlockSpec((rc, D_MODEL), lambda l, j: (0, 0)),
                  pl.BlockSpec((1, D_MODEL, tn), lambda l, j: (l, 0, j)),
                  pl.BlockSpec((1, 1, tn), lambda l, j: (l, 0, j))],
        out_specs=pl.BlockSpec((1, rc, tn), lambda l, j: (l, 0, j)),
        compiler_params=_cparams("arbitrary", "arbitrary"),
        name="ada_ln",
    )(c_all, w_ada, b_ada.reshape(depth, 1, n))


def _mod_spec(per_token, tm, rows_per_batch_tile, chunk):
    if per_token:
        return pl.BlockSpec((tm, D_MODEL), lambda i: (i, chunk))
    return pl.BlockSpec((None, 1, D_MODEL), lambda i: (i // rows_per_batch_tile, 0, chunk))


def _proj_kernel(x_ref, g_ref, sh_ref, sc_ref, w_ref, cf_ref, sf_ref, ck_ref, sk_ref,
                 dq_ref, dkv_ref, mq_ref, mkv_ref, nq_ref, n4_ref, nwin_ref, ng_ref):
    h = _mod_norm(x_ref[...], g_ref[0], sh_ref[...], sc_ref[...])
    y = _dot(h, w_ref[0])
    cf, sf, ck, sk = cf_ref[...], sf_ref[...], ck_ref[...], sk_ref[...]
    dq_ref[...] = (_rope(y[:, 0:512], cf, sf) * Q_SCALE).astype(dq_ref.dtype)
    dkv_ref[:, 0:512] = _rope(y[:, 512:1024], cf, sf)
    dkv_ref[:, 512:1024] = y[:, 1024:1536]
    mq_ref[...] = (_rope(y[:, 1536:1792], cf, sf) * Q_SCALE).astype(mq_ref.dtype)
    mkv_ref[:, 0:256] = _rope(y[:, 1792:2048], cf, sf)
    mkv_ref[:, 256:512] = y[:, 2048:2304]
    nq_ref[...] = _rope(y[:, 2304:2560], cf, sf) * Q_SCALE
    n4_ref[...] = _rope(y[:, 2560:2816], ck, sk)
    nwin_ref[...] = _rope(y[:, 2816:2944], ck, sk)
    ng_ref[...] = _sigmoid(y[:, 2944:3072])


def project(x, layer, g_mix, shift, scale, w1, tables, *, tm, rows_per_batch, per_token):
    r = x.shape[0]
    nt = rows_per_batch // tm
    row = lambda w: pl.BlockSpec((tm, w), lambda i: (i, 0))
    tab = pl.BlockSpec((tm, LANES), lambda i: (i % nt, 0))
    out_shapes = (
        jax.ShapeDtypeStruct((r, 512), MXU_DTYPE),
        jax.ShapeDtypeStruct((r, 1024), F32),
        jax.ShapeDtypeStruct((r, 256), MXU_DTYPE),
        jax.ShapeDtypeStruct((r, 512), F32),
        jax.ShapeDtypeStruct((r, 256), F32),
        jax.ShapeDtypeStruct((r, 256), F32),
        jax.ShapeDtypeStruct((r, 128), F32),
        jax.ShapeDtypeStruct((r, 128), F32),
    )
    return pl.pallas_call(
        _proj_kernel,
        out_shape=out_shapes,
        grid=(r // tm,),
        in_specs=[row(D_MODEL),
                  pl.BlockSpec((1, 1, D_MODEL), lambda i: (layer, 0, 0)),
                  _mod_spec(per_token, tm, nt, 0), _mod_spec(per_token, tm, nt, 1),
                  pl.BlockSpec((1, D_MODEL, PROJ_PAD), lambda i: (layer, 0, 0)),
                  tab, tab, tab, tab],
        out_specs=tuple(row(s.shape[1]) for s in out_shapes),
        compiler_params=_cparams("arbitrary"),
        name="project",
    )(x, g_mix, shift, scale, w1, *tables)


def rope_tables(pos):
    freqs = ROPE_THETA ** (-jnp.arange(HALF, dtype=F32) / HALF)
    ang = pos.astype(F32)[:, None] * freqs[None, :]
    cos, sin = jnp.cos(ang), jnp.sin(ang)
    c64 = jnp.concatenate([cos, cos], axis=1)
    s64 = jnp.concatenate([-sin, sin], axis=1)
    one, zero = jnp.ones_like(c64), jnp.zeros_like(s64)
    return (jnp.concatenate([c64, c64], axis=1), jnp.concatenate([s64, s64], axis=1),
            jnp.concatenate([c64, one], axis=1), jnp.concatenate([s64, zero], axis=1))


def _diff_lambda(lam_ref, lam_init):
    lp = lam_ref[0]
    a = jnp.sum(lp[0:1] * lp[1:2], axis=-1, keepdims=True)
    b = jnp.sum(lp[2:3] * lp[3:4], axis=-1, keepdims=True)
    return jnp.exp(a) - jnp.exp(b) + lam_init


def _fill_kv(k_ref, v_ref, kb, vt):
    kb[...] = k_ref[...].astype(kb.dtype)
    tk = vt.shape[2]
    for c in range(vt.shape[0]):
        vt[c] = v_ref[c * tk:(c + 1) * tk, :].T.astype(vt.dtype)


def _split_heads_t(q_ref):
    qt = q_ref[...].astype(F32).T
    top = _lane(qt.shape, 0) < HEAD_DIM
    return jnp.where(top, qt, 0.0).astype(MXU_DTYPE), jnp.where(top, 0.0, qt).astype(MXU_DTYPE)


def _diff_prompt_kernel(q_ref, k_ref, v_ref, lam_ref, g_ref, o_ref,
                        kb, vt, m1, l1, a1, m2, l2, a2, *, tq, tk, lam_init):
    i = pl.program_id(2)

    @pl.when(i == 0)
    def _():
        _fill_kv(k_ref, v_ref, kb, vt)

    qs = _split_heads_t(q_ref)
    q0 = i * tq
    jd = q0 // tk

    def scores(j, diag):
        j0 = pl.multiple_of(j * tk, tk)
        kj = kb[pl.ds(j0, tk), :]
        out = [_dot(kj, qm) for qm in qs]
        if diag:
            causal = j0 + _lane((tk, tq), 0) <= q0 + _lane((tk, tq), 1)
            out = [jnp.where(causal, s, NEG) for s in out]
        return out

    _flash_t_loop(scores, vt, ((m1, l1, a1), (m2, l2, a2)), None, jd, 0, jd)

    lam = _diff_lambda(lam_ref, lam_init)
    o = a1[...] / l1[...] - lam * (a2[...] / l2[...])
    o = o * lax.rsqrt(jnp.mean(o * o, axis=0, keepdims=True) + RMS_EPS) * g_ref[0] * (1.0 - lam_init)
    o_ref[...] = o.T.astype(o_ref.dtype)


def diff_prompt(dq, dkv, diff_lambda, diff_norm_g, layer, *, batch, seq, tq=256, tk=512):
    nq = seq // tq
    lam_init = 0.8 - 0.6 * math.exp(-0.3 * layer)
    kern = functools.partial(_diff_prompt_kernel, tq=tq, tk=tk, lam_init=lam_init)
    stat = pltpu.VMEM((1, tq), F32)
    acc = pltpu.VMEM((DIFF_VDIM, tq), F32)
    return pl.pallas_call(
        kern,
        out_shape=jax.ShapeDtypeStruct((batch * seq, DIFF_HEADS * DIFF_VDIM), MXU_DTYPE),
        grid=(batch, DIFF_HEADS, nq),
        in_specs=[pl.BlockSpec((tq, 128), lambda b, h, i: (b * nq + i, h)),
                  pl.BlockSpec((seq, 128), lambda b, h, i: (b, h)),
                  pl.BlockSpec((seq, 128), lambda b, h, i: (b, DIFF_HEADS + h)),
                  pl.BlockSpec((1, 4, HEAD_DIM), lambda b, h, i: (layer, 0, 0)),
                  pl.BlockSpec((1, DIFF_VDIM, 1), lambda b, h, i: (layer, 0, 0))],
        out_specs=pl.BlockSpec((tq, 128), lambda b, h, i: (b * nq + i, h)),
        scratch_shapes=[pltpu.VMEM((seq, 128), MXU_DTYPE), pltpu.VMEM((seq // tk, 128, tk), MXU_DTYPE),
                        stat, stat, acc, stat, stat, acc],
        compiler_params=_cparams("arbitrary", "arbitrary", "arbitrary"),
        name="diff_prompt",
    )(dq, dkv, dkv, diff_lambda, diff_norm_g)


def _moba_prompt_kernel(q_ref, k_ref, v_ref, o_ref, kb, vt, kmean, sel1, sel2, sbuf, m1, l1, a1, m2, l2, a2,
                        *, n_blk, tk):
    i = pl.program_id(2)
    blk = MOBA_BLOCK
    per = tk // blk

    @pl.when(i == 0)
    def _():
        _fill_kv(k_ref, v_ref, kb, vt)
        for n in range(n_blk):
            kmean[n:n + 1, :] = jnp.sum(k_ref[n * blk:(n + 1) * blk, :], axis=0, keepdims=True) * (1.0 / blk)

    qs = _split_heads_t(q_ref)
    sels = (sel1, sel2)
    km = kmean[...]
    brow = _lane((n_blk, blk), 0)
    for qm, sel_ref in zip(qs, sels):
        gate = jnp.where(brow < i, _dot(km, qm), -jnp.inf)
        sel_ref[...] = jnp.where(brow < i, _top_k_rows(gate, MOBA_TOPK), 0.0)

    causal = jnp.where(_lane((blk, blk), 0) <= _lane((blk, blk), 1), 1.0, 0.0)

    def scores(j, _):
        kj = kb[pl.ds(pl.multiple_of(j * tk, tk), tk), :]
        raw = [_dot(kj, qm) for qm in qs]
        out = []
        for s, sel_ref in zip(raw, sels):
            on = []
            for u in range(per):
                n = j * per + u
                past = jnp.broadcast_to(sel_ref[pl.ds(n, 1), :], (blk, blk))
                on.append(jnp.where(n == i, causal, jnp.where(n < i, past, 0.0)))
            out.append(jnp.where(jnp.concatenate(on, axis=0) > 0.5, s, NEG))
        return out

    jd = i // per
    _flash_t_loop(scores, vt, ((m1, l1, a1), (m2, l2, a2)), sbuf, jd, 0, jd)
    o = jnp.where(_lane((128, blk), 0) < HEAD_DIM, a1[...] / l1[...], a2[...] / l2[...])
    o_ref[...] = o.T.astype(o_ref.dtype)


def moba_prompt(mq, mkv, *, batch, seq, tk=512):
    blk = MOBA_BLOCK
    nq = seq // blk
    assert seq % tk == 0
    kern = functools.partial(_moba_prompt_kernel, n_blk=nq, tk=tk)
    stat = pltpu.VMEM((1, blk), F32)
    acc = pltpu.VMEM((128, blk), F32)
    sel = pltpu.VMEM((nq, blk), F32)
    pairs = MOBA_HEADS // 2
    return pl.pallas_call(
        kern,
        out_shape=jax.ShapeDtypeStruct((batch * seq, MOBA_HEADS * HEAD_DIM), MXU_DTYPE),
        grid=(batch, pairs, nq),
        in_specs=[pl.BlockSpec((blk, 128), lambda b, h, i: (b * nq + i, h)),
                  pl.BlockSpec((seq, 128), lambda b, h, i: (b, h)),
                  pl.BlockSpec((seq, 128), lambda b, h, i: (b, pairs + h))],
        out_specs=pl.BlockSpec((blk, 128), lambda b, h, i: (b * nq + i, h)),
        scratch_shapes=[pltpu.VMEM((seq, 128), MXU_DTYPE), pltpu.VMEM((seq // tk, 128, tk), MXU_DTYPE),
                        pltpu.VMEM((nq, 128), F32), sel, sel, pltpu.VMEM((2, 2, tk, blk), F32),
                        stat, stat, acc, stat, stat, acc],
        compiler_params=_cparams("arbitrary", "arbitrary", "arbitrary"),
        name="moba_prompt",
    )(mq, mkv, mkv)


def compress_weights(cmp_pos, cmp_w1, cmp_w2):
    depth = cmp_w1.shape[0]
    w1k = cmp_w1[:, 0].reshape(depth, CMP_LEN, HEAD_DIM, CMP_HIDDEN)
    w1v = cmp_w1[:, 1].reshape(depth, CMP_LEN, HEAD_DIM, CMP_HIDDEN)
    z = jnp.zeros_like(w1k)
    per_row = jnp.concatenate([jnp.concatenate([w1k, z], axis=-1), jnp.concatenate([z, w1v], axis=-1)], axis=2)
    w1p = per_row.reshape(depth, CMP_LEN // 2, 4 * HEAD_DIM, 2 * CMP_HIDDEN).astype(MXU_DTYPE)
    posrows = jnp.transpose(cmp_pos, (0, 2, 1, 3)).reshape(depth, CMP_LEN, 1, 2 * HEAD_DIM)
    z2 = jnp.zeros_like(cmp_w2[:, 0])
    w2p = jnp.concatenate([jnp.concatenate([cmp_w2[:, 0], z2], axis=-1),
                           jnp.concatenate([z2, cmp_w2[:, 1]], axis=-1)], axis=1).astype(MXU_DTYPE)
    return posrows, w1p, w2p


def _compress_rows(xs, n_tok, pos_ref, w1_ref, w2_ref):
    acc = jnp.zeros((n_tok, 2 * CMP_HIDDEN), F32)
    for p in range(CMP_LEN // 2):
        xa = xs[pl.ds(2 * p, n_tok, stride=CMP_STRIDE), :] + pos_ref[0, 2 * p]
        xb = xs[pl.ds(2 * p + 1, n_tok, stride=CMP_STRIDE), :] + pos_ref[0, 2 * p + 1]
        acc = acc + _dot(jnp.concatenate([xa, xb], axis=1), w1_ref[0, p])
    return _dot(_gelu_tanh(acc), w2_ref[0])


def _compress_prompt_kernel(x_ref, pos_ref, w1_ref, w2_ref, o_ref, xs, *, seq, n_tok):
    xs[0:seq, :] = x_ref[...]
    xs[seq:seq + CMP_LEN, :] = jnp.zeros((CMP_LEN, 128), F32)
    o_ref[...] = _compress_rows(xs, n_tok, pos_ref, w1_ref, w2_ref)


def compress_prompt(n4, posrows, w1p, w2p, layer, *, batch, seq):
    n_tok = seq // CMP_STRIDE
    kern = functools.partial(_compress_prompt_kernel, seq=seq, n_tok=n_tok)
    return pl.pallas_call(
        kern,
        out_shape=jax.ShapeDtypeStruct((batch, n_tok, 128), F32),
        grid=(batch,),
        in_specs=[pl.BlockSpec((seq, 128), lambda b: (b, 0)),
                  pl.BlockSpec((1, CMP_LEN, 1, 128), lambda b: (layer, 0, 0, 0)),
                  pl.BlockSpec((1, CMP_LEN // 2, 256, 512), lambda b: (layer, 0, 0, 0)),
                  pl.BlockSpec((1, 512, 128), lambda b: (layer, 0, 0))],
        out_specs=pl.BlockSpec((None, n_tok, 128), lambda b: (b, 0, 0)),
        scratch_shapes=[pltpu.VMEM((seq + CMP_LEN, 128), F32)],
        compiler_params=_cparams("arbitrary"),
        name="compress_prompt",
    )(n4, posrows, w1p, w2p)


def _cmp_to_sel_weights(n_cmp, n_cmp_pad, n_sel_pad):
    ratio_sel = SEL_BLOCK // CMP_STRIDE
    ratio_cmp = CMP_LEN // CMP_STRIDE
    off = np.arange(n_cmp)[:, None] - ratio_sel * np.arange(n_sel_pad)[None, :]
    shifts = (np.arange(ratio_sel)[:, None] - np.arange(ratio_cmp)[None, :]).reshape(-1)
    w = np.sum(off[..., None] == shifts, axis=-1).astype(np.float32)
    return np.pad(w, ((0, n_cmp_pad - n_cmp), (0, 0)))


def _nsa_prompt_kernel(q_ref, kvc_ref, sel_ref, win_ref, ng_ref, wselt_ref, o_ref,
                       kvc_b, kvc_t, sel_b, sel_t, win_b, win_t, selm, sbuf, m_s, l_s, a_s, m_w, l_w, a_w,
                       *, tq, tk, n_cmp):
    i = pl.program_id(1)
    nh = NSA_HEADS

    @pl.when(i == 0)
    def _():
        kvc = kvc_ref[...]
        kvc_b[...] = kvc.astype(kvc_b.dtype)
        kvc_t[...] = kvc.T.astype(kvc_t.dtype)
        _fill_kv(sel_ref, sel_ref, sel_b, sel_t)
        _fill_kv(win_ref, win_ref, win_b, win_t)

    qt = q_ref[...].T
    zero = jnp.zeros((HEAD_DIM, tq), F32)
    qp = jnp.concatenate([jnp.concatenate([qt[HEAD_DIM * h:HEAD_DIM * (h + 1)], zero], axis=0) for h in range(nh)],
                         axis=1).astype(MXU_DTYPE)
    cols = nh * tq
    q0 = i * tq
    pos = q0 + (_lane((1, cols)) & (tq - 1))
    pos_q = q0 + _lane((1, tq))

    s = _dot(kvc_b[...], qp)
    n = _lane(s.shape, 0)
    vis = jnp.where(n * CMP_STRIDE + (CMP_LEN - 1) <= pos, n, n_cmp) < n_cmp
    s = jnp.where(vis, s, NEG)
    e = jnp.where(vis, jnp.exp(s - jnp.max(s, axis=0, keepdims=True)), 0.0)
    den = jnp.sum(e, axis=0, keepdims=True)
    p = (e / jnp.where(den > 0, den, 1.0)).astype(MXU_DTYPE)
    o_cmp = _dot(kvc_t[...], p)
    wselt = wselt_ref[...]
    imp = _dot(wselt, p[:, 0:tq])
    for h in range(1, nh):
        imp = imp + _dot(wselt, p[:, h * tq:(h + 1) * tq])

    own = lax.shift_right_logical(pos_q, int(math.log2(SEL_BLOCK)))
    jb = _lane(imp.shape, 0)
    forced = jnp.where(jb == 0, 1, jnp.where(jb == own, 1, jnp.where(jb == own - 1, 1, 0))) > 0
    impm = jnp.where(jb <= own, jnp.where(forced, jnp.inf, imp), -jnp.inf)
    selm[...] = jnp.where(jb <= own, _top_k_rows(impm, SEL_TOPK), 0.0)

    jd = q0 // tk
    per = tk // SEL_BLOCK

    def sel_scores(j, diag):
        j0 = pl.multiple_of(j * tk, tk)
        s = _dot(sel_b[pl.ds(j0, tk), :], qp)
        on = jnp.concatenate([jnp.broadcast_to(selm[pl.ds(j * per + u, 1), :], (SEL_BLOCK, tq)) for u in range(per)],
                             axis=0)
        on = jnp.concatenate([on] * nh, axis=1)
        if diag:
            on = jnp.where(j0 + _lane((tk, cols), 0) <= pos, on, 0.0)
        return [jnp.where(on > 0.5, s, NEG)]

    _flash_t_loop(sel_scores, sel_t, ((m_s, l_s, a_s),), sbuf, jd, 0, jd)

    def win_scores(j, _):
        j0 = pl.multiple_of(j * tk, tk)
        s = _dot(win_b[pl.ds(j0, tk), :], qp)
        key = j0 + _lane((tk, cols), 0)
        on = jnp.where(key <= pos, jnp.where(key > pos - WINDOW, 1, 0), 0) > 0
        return [jnp.where(on, s, NEG)]

    _flash_t_loop(win_scores, win_t, ((m_w, l_w, a_w),), sbuf, jd, jnp.maximum(q0 - (WINDOW - 1), 0) // tk, jd)

    n_pad = jnp.maximum(WINDOW - 1 - pos, 0).astype(F32)
    m_win = m_w[...]
    m_all = jnp.where(n_pad > 0, jnp.maximum(m_win, 0.0), m_win)
    shrink = jnp.exp(m_win - m_all)
    o_win = (a_w[...] * shrink) / (l_w[...] * shrink + n_pad * jnp.exp(-m_all))

    gt = ng_ref[...].T

    def gate(c):
        return jnp.concatenate([gt[3 * h + c:3 * h + c + 1, :] for h in range(nh)], axis=1)

    o = gate(0) * o_cmp + gate(1) * (a_s[...] / l_s[...]) + gate(2) * o_win
    o = jnp.where(_lane(o.shape, 0) >= HEAD_DIM, o, 0.0)
    o_ref[...] = jnp.concatenate([o[:, h * tq:(h + 1) * tq].T for h in range(nh)], axis=1).astype(o_ref.dtype)


def nsa_prompt(nq, kvc, n4, nwin, ng, *, batch, seq, tq=128, tk=512):
    nqb = seq // tq
    n_cmp = (seq - CMP_LEN) // CMP_STRIDE + 1
    n_cmp_pad = kvc.shape[1]
    wselt = jnp.asarray(_cmp_to_sel_weights(n_cmp, n_cmp_pad, 128).T, MXU_DTYPE)
    cols = NSA_HEADS * tq
    kern = functools.partial(_nsa_prompt_kernel, tq=tq, tk=tk, n_cmp=n_cmp)
    stat = pltpu.VMEM((1, cols), F32)
    acc = pltpu.VMEM((128, cols), F32)
    rows = pltpu.VMEM((seq, 128), MXU_DTYPE)
    trans = pltpu.VMEM((seq // tk, 128, tk), MXU_DTYPE)
    return pl.pallas_call(
        kern,
        out_shape=jax.ShapeDtypeStruct((batch * seq, NSA_HEADS * 128), MXU_DTYPE),
        grid=(batch, nqb),
        in_specs=[pl.BlockSpec((tq, 256), lambda b, i: (b * nqb + i, 0)),
                  pl.BlockSpec((None, n_cmp_pad, 128), lambda b, i: (b, 0, 0)),
                  pl.BlockSpec((seq, 128), lambda b, i: (b, 1)),
                  pl.BlockSpec((seq, 128), lambda b, i: (b, 0)),
                  pl.BlockSpec((tq, 128), lambda b, i: (b * nqb + i, 0)),
                  pl.BlockSpec((128, n_cmp_pad), lambda b, i: (0, 0))],
        out_specs=pl.BlockSpec((tq, NSA_HEADS * 128), lambda b, i: (b * nqb + i, 0)),
        scratch_shapes=[pltpu.VMEM((n_cmp_pad, 128), MXU_DTYPE), pltpu.VMEM((128, n_cmp_pad), MXU_DTYPE),
                        rows, trans, rows, trans, pltpu.VMEM((128, tq), F32), pltpu.VMEM((2, 1, tk, cols), F32),
                        stat, stat, acc, stat, stat, acc],
        compiler_params=_cparams("arbitrary", "arbitrary"),
        name="nsa_prompt",
    )(nq, kvc, n4, nwin, ng, wselt)


def _merge_kernel(x_ref, oa_ref, ob_ref, oc_ref, g_ref, sh_ref, sc_ref, gt_ref,
                  wbg_ref, wa_ref, wb_ref, wc_ref, wo_ref, o_ref):
    x = x_ref[...]
    h = _mod_norm(x, g_ref[0], sh_ref[...], sc_ref[...])
    bg = _sigmoid(_dot(h, wbg_ref[0]))
    d = D_MODEL
    m = (bg[:, 0:d] * _dot(oa_ref[...], wa_ref[0]) + bg[:, d:2 * d] * _dot(ob_ref[...], wb_ref[0])
         + bg[:, 2 * d:3 * d] * _dot(oc_ref[...], wc_ref[0]))
    o_ref[...] = x + gt_ref[...] * _dot(m, wo_ref[0])


def merge(x, oa, ob, oc, layer, g_mix, mod, wbg, wa, wb, wc, wo, *, tm, rows_per_batch, per_token):
    r = x.shape[0]
    nt = rows_per_batch // tm
    row = lambda w: pl.BlockSpec((tm, w), lambda i: (i, 0))
    wspec = lambda a: pl.BlockSpec((1,) + a.shape[1:], lambda i: (layer, 0, 0))
    return pl.pallas_call(
        _merge_kernel,
        out_shape=jax.ShapeDtypeStruct((r, D_MODEL), F32),
        grid=(r // tm,),
        in_specs=[row(D_MODEL), row(oa.shape[1]), row(ob.shape[1]), row(oc.shape[1]),
                  pl.BlockSpec((1, 1, D_MODEL), lambda i: (layer, 0, 0)),
                  _mod_spec(per_token, tm, nt, 0), _mod_spec(per_token, tm, nt, 1), _mod_spec(per_token, tm, nt, 2),
                  wspec(wbg), wspec(wa), wspec(wb), wspec(wc), wspec(wo)],
        out_specs=row(D_MODEL),
        compiler_params=_cparams("arbitrary"),
        name="merge",
    )(x, oa, ob, oc, g_mix, mod, mod, mod, wbg, wa, wb, wc, wo)


def _route(h, rw_ref, rb_ref):
    logits = jnp.dot(h, rw_ref[...], precision=lax.Precision.HIGHEST, preferred_element_type=F32)
    aff = _sigmoid(logits)
    biased = aff + rb_ref[...]
    lane = _lane(aff.shape)
    grp = lax.shift_right_logical(lane, int(math.log2(EXPERTS_PER_GROUP)))
    sentinel = jnp.int32(aff.shape[-1])
    best = None
    for gi in range(N_GROUPS):
        v1 = jnp.where(grp == gi, biased, -jnp.inf)
        m1 = jnp.max(v1, axis=-1, keepdims=True)
        i1 = jnp.min(jnp.where(v1 == m1, lane, sentinel), axis=-1, keepdims=True)
        v2 = jnp.where(lane == i1, -jnp.inf, v1)
        m2 = jnp.max(v2, axis=-1, keepdims=True)
        i2 = jnp.min(jnp.where(v2 == m2, lane, sentinel), axis=-1, keepdims=True)
        cand = (m1 + m2, i1, i2)
        if best is None:
            best = cand
        else:
            better = cand[0] > best[0]
            best = tuple(jnp.where(better, c, b) for c, b in zip(cand, best))
    _, i1, i2 = best
    w1 = jnp.sum(jnp.where(lane == i1, aff, 0.0), axis=-1, keepdims=True)
    w2 = jnp.sum(jnp.where(lane == i2, aff, 0.0), axis=-1, keepdims=True)
    tot = w1 + w2
    return jnp.where(lane == i1, w1 / tot, 0.0) + jnp.where(lane == i2, w2 / tot, 0.0)


def _moe_kernel(x_ref, g_ref, sh_ref, sc_ref, gt_ref, rw_ref, rb_ref, wg_ref, wu_ref, wd_ref, gf_ref,
                o_ref, hb, gate, acc, *, final):
    e = pl.program_id(1)

    @pl.when(e == 0)
    def _():
        h = _mod_norm(x_ref[...], g_ref[0], sh_ref[...], sc_ref[...])
        hb[...] = h.astype(hb.dtype)
        gate[...] = _route(h, rw_ref, rb_ref)
        acc[...] = jnp.zeros_like(acc)

    h = hb[...]
    gv = gate[...]
    ge = jnp.sum(jnp.where(_lane(gv.shape) == e, gv, 0.0), axis=-1, keepdims=True)
    hid = _silu(_dot(h, wg_ref[0, 0])) * _dot(h, wu_ref[0, 0]) * ge
    acc[...] += _dot(hid, wd_ref[0, 0])

    @pl.when(e == N_EXPERTS - 1)
    def _():
        y = x_ref[...] + gt_ref[...] * acc[...]
        if final:
            y = _rms(y) * gf_ref[...]
        o_ref[...] = y


def moe(x, layer, g_ffn, mod, rw, rb, wg, wu, wd, g_final, *, tm, rows_per_batch, per_token, final):
    r = x.shape[0]
    nt = rows_per_batch // tm
    row = pl.BlockSpec((tm, D_MODEL), lambda i, e: (i, 0))
    if per_token:
        mspec = lambda c: pl.BlockSpec((tm, D_MODEL), lambda i, e: (i, c))
    else:
        mspec = lambda c: pl.BlockSpec((None, 1, D_MODEL), lambda i, e: (i // nt, 0, c))
    kern = functools.partial(_moe_kernel, final=final)
    return pl.pallas_call(
        kern,
        out_shape=jax.ShapeDtypeStruct((r, D_MODEL), F32),
        grid=(r // tm, N_EXPERTS),
        in_specs=[row, pl.BlockSpec((1, 1, D_MODEL), lambda i, e: (layer, 0, 0)),
                  mspec(3), mspec(4), mspec(5),
                  pl.BlockSpec((D_MODEL, 128), lambda i, e: (0, 0)),
                  pl.BlockSpec((1, 128), lambda i, e: (0, 0)),
                  pl.BlockSpec((1, 1, D_MODEL, EXPERT_FF), lambda i, e: (layer, e, 0, 0)),
                  pl.BlockSpec((1, 1, D_MODEL, EXPERT_FF), lambda i, e: (layer, e, 0, 0)),
                  pl.BlockSpec((1, 1, EXPERT_FF, D_MODEL), lambda i, e: (layer, e, 0, 0)),
                  pl.BlockSpec((1, D_MODEL), lambda i, e: (0, 0))],
        out_specs=row,
        scratch_shapes=[pltpu.VMEM((tm, D_MODEL), MXU_DTYPE), pltpu.VMEM((tm, 128), F32),
                        pltpu.VMEM((tm, D_MODEL), F32)],
        compiler_params=_cparams("arbitrary", "arbitrary"),
        name="moe",
    )(x, g_ffn, mod, mod, mod, rw, rb, wg, wu, wd, g_final)


def _page_specs(n_pages, pages_per_step, layer, block):
    zeros = (0,) * len(block)

    def spec(p):
        return pl.BlockSpec((None, None) + block,
                            lambda b, j, pt: (layer, pt[b * n_pages + j * pages_per_step + p]) + zeros)
    return [spec(p) for p in range(pages_per_step)]


def _new_token_mask(rows):
    t = lax.broadcasted_iota(jnp.int32, (rows, 8), 0) & (N_NEW - 1)
    tk = _lane((rows, 8))
    return jnp.where(tk <= t, jnp.where(tk < N_NEW, 1, 0), 0) > 0


def _diff_sample_kernel(pt_ref, q_ref, *rest, n_pg, lam_init):
    pages = rest[:n_pg]
    new_ref, lam_ref, g_ref, o_ref, m_ref, l_ref, a_ref = rest[n_pg:]
    j = pl.program_id(1)
    nh = DIFF_HEADS

    @pl.when(j == 0)
    def _():
        m_ref[...] = jnp.full(m_ref.shape, NEG, F32)
        l_ref[...] = jnp.zeros(l_ref.shape, F32)
        a_ref[...] = jnp.zeros(a_ref.shape, F32)

    qh = [q_ref[h] for h in range(nh)]

    def keys(pg, h):
        return pg[pl.ds(h, PAGE_SIZE, stride=2 * nh), :].astype(MXU_DTYPE)

    def values(pg, h):
        return pg[pl.ds(nh + h, PAGE_SIZE, stride=2 * nh), :].astype(MXU_DTYPE)

    s = jnp.concatenate([jnp.concatenate([_dot_nt(qh[h], keys(pg, h)) for h in range(nh)], axis=0) for pg in pages],
                        axis=1)

    def weighted_values(p):
        acc = None
        for pi, pg in enumerate(pages):
            pp = p[:, pi * PAGE_SIZE:(pi + 1) * PAGE_SIZE]
            part = jnp.concatenate([_dot(pp[8 * h:8 * h + 8], values(pg, h)) for h in range(nh)], axis=0)
            acc = part if acc is None else acc + part
        return acc

    _flash_next(s, weighted_values, m_ref, l_ref, a_ref)

    @pl.when(j == pl.num_programs(1) - 1)
    def _():
        new = new_ref[...]
        s = jnp.concatenate([_dot_nt(qh[h], new[:, 128 * h:128 * h + 128]) for h in range(nh)], axis=0)
        s = jnp.where(_new_token_mask(8 * nh), s, NEG)
        _flash_next(s, lambda p: jnp.concatenate(
            [_dot(p[8 * h:8 * h + 8], new[:, 512 + 128 * h:640 + 128 * h]) for h in range(nh)], axis=0),
            m_ref, l_ref, a_ref)
        a = a_ref[...] / l_ref[...]
        lam = _diff_lambda(lam_ref, lam_init)
        outs = []
        for h in range(nh):
            grp = a[8 * h:8 * h + 8]
            o = grp - lam * pltpu.roll(grp, 4, 0)
            outs.append(_rms(o) * g_ref[0] * (1.0 - lam_init))
        o_ref[...] = jnp.concatenate(outs, axis=1).astype(o_ref.dtype)


def diff_sample(page_table, qrows, cache, new_kv, diff_lambda, diff_norm_g, layer, *, pages_per_step=16):
    n_seq, n_pages = page_table.shape
    lam_init = 0.8 - 0.6 * math.exp(-0.3 * layer)
    kern = functools.partial(_diff_sample_kernel, n_pg=pages_per_step, lam_init=lam_init)
    rows = 8 * DIFF_HEADS
    grid_spec = pltpu.PrefetchScalarGridSpec(
        num_scalar_prefetch=1,
        grid=(n_seq, n_pages // pages_per_step),
        in_specs=[pl.BlockSpec((None, DIFF_HEADS, 8, 128), lambda b, j, pt: (b, 0, 0, 0))]
        + _page_specs(n_pages, pages_per_step, layer, (2 * DIFF_HEADS * PAGE_SIZE, 128))
        + [pl.BlockSpec((None, 8, 1024), lambda b, j, pt: (b, 0, 0)),
           pl.BlockSpec((1, 4, HEAD_DIM), lambda b, j, pt: (layer, 0, 0)),
           pl.BlockSpec((1, 1, DIFF_VDIM), lambda b, j, pt: (layer, 0, 0))],
        out_specs=pl.BlockSpec((None, 8, 512), lambda b, j, pt: (b, 0, 0)),
        scratch_shapes=[pltpu.VMEM((rows, 1), F32), pltpu.VMEM((rows, 1), F32), pltpu.VMEM((rows, DIFF_VDIM), F32)],
    )
    return pl.pallas_call(
        kern,
        out_shape=jax.ShapeDtypeStruct((n_seq, 8, 512), MXU_DTYPE),
        grid_spec=grid_spec,
        compiler_params=_cparams("arbitrary", "arbitrary"),
        name="diff_sample",
    )(page_table.reshape(-1), qrows, *([cache] * pages_per_step), new_kv, diff_lambda, diff_norm_g)


def _moba_sample_kernel(pt_ref, q_ref, *rest, n_pg, n_blk):
    pages = rest[:n_pg]
    nk_ref, nv_ref, o_ref, m_all, l_all, acc_all, kmean = rest[n_pg:]
    j = pl.program_id(1)
    nh = MOBA_HEADS

    @pl.when(j == 0)
    def _():
        m_all[...] = jnp.full(m_all.shape, NEG, F32)
        l_all[...] = jnp.zeros(l_all.shape, F32)
        kmean[...] = jnp.zeros(kmean.shape, F32)

    qh = [q_ref[h] for h in range(nh)]
    lane = _lane(m_all.shape)
    klane = _lane((HEAD_DIM, LANES))
    for b2 in range(n_pg // 2):
        n = j * (n_pg // 2) + b2
        pa, pb = pages[2 * b2], pages[2 * b2 + 1]
        s, pv = [], []
        for h in range(nh):
            kt = jnp.concatenate([pa[0, h], pb[0, h]], axis=1)
            s.append(_dot(qh[h], kt))
            kmean[h] = jnp.where(klane == n, jnp.sum(kt, axis=1, keepdims=True) * (1.0 / MOBA_BLOCK), kmean[h])
        s = jnp.concatenate(s, axis=0)
        m = jnp.max(s, axis=-1, keepdims=True)
        p = jnp.exp(s - m)
        m_all[...] = jnp.where(lane == n, m, m_all[...])
        l_all[...] = jnp.where(lane == n, jnp.sum(p, axis=-1, keepdims=True), l_all[...])
        for h in range(nh):
            vt = jnp.concatenate([pa[1, h], pb[1, h]], axis=1)
            pv.append(_dot_nt(p[8 * h:8 * h + 8], vt))
        acc_all[n] = jnp.concatenate(pv, axis=0)

    @pl.when(j == pl.num_programs(1) - 1)
    def _():
        gate = jnp.concatenate([_dot(qh[h], kmean[h]) for h in range(nh)], axis=0)
        sel = jnp.where(lane < n_blk, _top_k_mask(gate, MOBA_TOPK, n_blk), 0.0) > 0.5
        s_new = jnp.concatenate([_dot_nt(qh[h], nk_ref[h]) for h in range(nh)], axis=0)
        s_new = jnp.where(_new_token_mask(8 * nh), s_new, NEG)
        m_new = jnp.max(s_new, axis=-1, keepdims=True)
        p_new = jnp.exp(s_new - m_new)
        m_tot = jnp.maximum(m_new, jnp.max(jnp.where(sel, m_all[...], NEG), axis=-1, keepdims=True))
        w = jnp.where(sel, jnp.exp(m_all[...] - m_tot), 0.0)
        w_new = jnp.exp(m_new - m_tot)
        l_tot = jnp.sum(w * l_all[...], axis=-1, keepdims=True) + w_new * jnp.sum(p_new, axis=-1, keepdims=True)
        a_new = jnp.concatenate([_dot(p_new[8 * h:8 * h + 8], nv_ref[h]) for h in range(nh)], axis=0)

        def body(n, acc):
            wn = jnp.sum(jnp.where(lane == n, w, 0.0), axis=-1, keepdims=True)
            return acc + wn * acc_all[n]

        o_ref[...] = (lax.fori_loop(0, n_blk, body, w_new * a_new) / l_tot).astype(o_ref.dtype)


def moba_sample(page_table, qrows, cache, new_k, new_v, layer, *, pages_per_step=16):
    n_seq, n_pages = page_table.shape
    n_blk = n_pages * PAGE_SIZE // MOBA_BLOCK
    nh = MOBA_HEADS
    kern = functools.partial(_moba_sample_kernel, n_pg=pages_per_step, n_blk=n_blk)
    per_head = pl.BlockSpec((None, nh, 8, HEAD_DIM), lambda b, j, pt: (b, 0, 0, 0))
    grid_spec = pltpu.PrefetchScalarGridSpec(
        num_scalar_prefetch=1,
        grid=(n_seq, n_pages // pages_per_step),
        in_specs=[per_head] + _page_specs(n_pages, pages_per_step, layer, (2, nh, HEAD_DIM, PAGE_SIZE))
        + [per_head, per_head],
        out_specs=pl.BlockSpec((None, 8 * nh, HEAD_DIM), lambda b, j, pt: (b, 0, 0)),
        scratch_shapes=[pltpu.VMEM((8 * nh, LANES), F32), pltpu.VMEM((8 * nh, LANES), F32),
                        pltpu.VMEM((n_blk, 8 * nh, HEAD_DIM), F32), pltpu.VMEM((nh, HEAD_DIM, LANES), F32)],
    )
    return pl.pallas_call(
        kern,
        out_shape=jax.ShapeDtypeStruct((n_seq, 8 * nh, HEAD_DIM), MXU_DTYPE),
        grid_spec=grid_spec,
        compiler_params=_cparams("arbitrary", "arbitrary"),
        name="moba_sample",
    )(page_table.reshape(-1), qrows, *([cache] * pages_per_step), new_k, new_v)


def _nsa_sample_kernel(pt_ref, qa_ref, *rest, n_pg, past, n_cmp):
    pages = rest[:n_pg]
    (win_ref, new_ref, ng_ref, pos_ref, w1_ref, w2_ref, wsel_ref, o_ref, nwin_ref,
     xs, m_all, l_all, acc_all) = rest[n_pg:]
    j = pl.program_id(1)
    n_selp = m_all.shape[1]

    @pl.when(j == 0)
    def _():
        xs[past:past + CMP_LEN, :] = jnp.zeros((CMP_LEN, 128), F32)
        m_all[...] = jnp.full(m_all.shape, NEG, F32)
        l_all[...] = jnp.zeros(l_all.shape, F32)

    qa = qa_ref[...]
    q64 = qa[:, 0:HEAD_DIM]
    lane_b = _lane(m_all.shape)
    low = _lane((16, PAGE_SIZE)) < SEL_BLOCK
    for pi, pg in enumerate(pages):
        row0 = pl.multiple_of((j * n_pg + pi) * PAGE_SIZE, PAGE_SIZE)
        xs[pl.ds(row0, PAGE_SIZE), :] = jnp.concatenate([pg[0], pg[1]], axis=0).T
        s = _dot(q64, pg[2])
        m_lo = jnp.max(jnp.where(low, s, NEG), axis=-1, keepdims=True)
        m_hi = jnp.max(jnp.where(low, NEG, s), axis=-1, keepdims=True)
        p = jnp.exp(s - jnp.where(low, m_lo, m_hi))
        p_lo, p_hi = jnp.where(low, p, 0.0), jnp.where(low, 0.0, p)
        n0 = 2 * (j * n_pg + pi)
        m_all[...] = jnp.where(lane_b == n0, m_lo, jnp.where(lane_b == n0 + 1, m_hi, m_all[...]))
        l_all[...] = jnp.where(lane_b == n0, jnp.sum(p_lo, axis=-1, keepdims=True),
                               jnp.where(lane_b == n0 + 1, jnp.sum(p_hi, axis=-1, keepdims=True), l_all[...]))
        pv = _dot_nt(jnp.concatenate([p_lo, p_hi], axis=0), pg[3])
        acc_all[n0] = pv[0:16]
        acc_all[n0 + 1] = pv[16:32]

    @pl.when(j == pl.num_programs(1) - 1)
    def _():
        new = new_ref[...]
        t16 = lax.broadcasted_iota(jnp.int32, (16, 1), 0) & (N_NEW - 1)
        new_ok = _new_token_mask(16)
        value = lambda o: pltpu.roll(o, HEAD_DIM, 1)[:, 0:HEAD_DIM]

        kvc = _compress_rows(xs, n_cmp + 1, pos_ref, w1_ref, w2_ref).astype(MXU_DTYPE)
        s = _dot_nt(qa, kvc)
        n = _lane(s.shape)
        vis = jnp.where(n * CMP_STRIDE + (CMP_LEN - 1) <= past + t16, n, n_cmp) < n_cmp
        s = jnp.where(vis, s, NEG)
        e = jnp.where(vis, jnp.exp(s - jnp.max(s, axis=-1, keepdims=True)), 0.0)
        den = jnp.sum(e, axis=-1, keepdims=True)
        p = (e / jnp.where(den > 0, den, 1.0)).astype(MXU_DTYPE)
        o_cmp = value(_dot(p, kvc))
        pw = _dot(p, wsel_ref[...])
        imp = pw[0:8] + pw[8:16]
        imp = imp + pltpu.roll(imp, 4, 0)

        own = lax.shift_right_logical(past + (lax.broadcasted_iota(jnp.int32, (8, 1), 0) & (N_NEW - 1)),
                                      int(math.log2(SEL_BLOCK)))
        jb = _lane(imp.shape)
        forced = jnp.where(jb == 0, 1, jnp.where(jb == own, 1, jnp.where(jb == own - 1, 1, 0))) > 0
        impm = jnp.where(jb <= own, jnp.where(forced, jnp.inf, imp), -jnp.inf)
        selm = jnp.where(jb <= own, _top_k_mask(impm, SEL_TOPK, imp.shape[-1]), 0.0)
        sel = jnp.concatenate([selm, selm], axis=0)[:, 0:n_selp] > 0.5

        s_new = jnp.where(new_ok, _dot_nt(qa, new[:, 128:256]), NEG)
        m_new = jnp.max(s_new, axis=-1, keepdims=True)
        p_new = jnp.exp(s_new - m_new)
        m_tot = jnp.maximum(m_new, jnp.max(jnp.where(sel, m_all[...], NEG), axis=-1, keepdims=True))
        w = jnp.where(sel, jnp.exp(m_all[...] - m_tot), 0.0)
        w_new = jnp.exp(m_new - m_tot)
        l_tot = jnp.sum(w * l_all[...], axis=-1, keepdims=True) + w_new * jnp.sum(p_new, axis=-1, keepdims=True)

        def body(nb, acc):
            wn = jnp.sum(jnp.where(lane_b == nb, w, 0.0), axis=-1, keepdims=True)
            return acc + wn * acc_all[nb]

        o_sel = lax.fori_loop(0, n_selp, body, w_new * value(_dot(p_new, new[:, 128:256]))) / l_tot

        win = win_ref[...]
        keep = win.shape[0]
        s_w = _dot_nt(qa, win)
        s_w = jnp.where(_lane(s_w.shape) > t16 + (keep - WINDOW), s_w, NEG)
        s_wn = jnp.where(new_ok, _dot_nt(qa, new[:, 256:384]), NEG)
        m_w = jnp.maximum(jnp.max(s_w, axis=-1, keepdims=True), jnp.max(s_wn, axis=-1, keepdims=True))
        p_w, p_wn = jnp.exp(s_w - m_w), jnp.exp(s_wn - m_w)
        l_w = jnp.sum(p_w, axis=-1, keepdims=True) + jnp.sum(p_wn, axis=-1, keepdims=True)
        o_win = value(_dot(p_w, win) + _dot(p_wn, new[:, 256:384])) / l_w

        g = ng_ref[...]
        top_rows = lax.broadcasted_iota(jnp.int32, (8, 1), 0) < 4

        def gate(c):
            col = lambda h: g[:, 3 * h + c:3 * h + c + 1]
            return jnp.concatenate([jnp.where(top_rows, col(0), col(1)), jnp.where(top_rows, col(2), col(3))], axis=0)

        o_ref[...] = (gate(0) * o_cmp + gate(1) * o_sel + gate(2) * o_win).astype(o_ref.dtype)

        nwin_ref[0:keep - N_NEW, :] = win_ref[N_NEW:keep, :]
        nwin_ref[keep - N_NEW:keep, :] = new[0:N_NEW, 256:384]


def nsa_sample(page_table, qa, cache, win_state, new_rows, ng8, posrows, w1p, w2p, layer, *, pages_per_step=16):
    n_seq, n_pages = page_table.shape
    past = n_pages * PAGE_SIZE
    keep = win_state.shape[2]
    total = past + N_NEW
    n_cmp = (total - CMP_LEN) // CMP_STRIDE + 1
    n_selp = past // SEL_BLOCK
    n_sel_pad = -(-(n_selp + 1) // LANES) * LANES
    wsel = jnp.asarray(_cmp_to_sel_weights(n_cmp, n_cmp + 1, n_sel_pad), MXU_DTYPE)
    kern = functools.partial(_nsa_sample_kernel, n_pg=pages_per_step, past=past, n_cmp=n_cmp)
    const = lambda shape: pl.BlockSpec(shape, lambda b, j, pt: (0,) * len(shape))
    rows = N_NEW * NSA_HEADS
    grid_spec = pltpu.PrefetchScalarGridSpec(
        num_scalar_prefetch=1,
        grid=(n_seq, n_pages // pages_per_step),
        in_specs=[pl.BlockSpec((None, rows, 128), lambda b, j, pt: (b, 0, 0))]
        + _page_specs(n_pages, pages_per_step, layer, (4, HEAD_DIM, PAGE_SIZE))
        + [pl.BlockSpec((None, None, keep, 128), lambda b, j, pt: (layer, b, 0, 0)),
           pl.BlockSpec((None, 8, 384), lambda b, j, pt: (b, 0, 0)),
           pl.BlockSpec((None, 8, 128), lambda b, j, pt: (b, 0, 0)),
           pl.BlockSpec((1, CMP_LEN, 1, 128), lambda b, j, pt: (layer, 0, 0, 0)),
           pl.BlockSpec((1, CMP_LEN // 2, 256, 512), lambda b, j, pt: (layer, 0, 0, 0)),
           pl.BlockSpec((1, 512, 128), lambda b, j, pt: (layer, 0, 0)),
           const((n_cmp + 1, n_sel_pad))],
        out_specs=(pl.BlockSpec((None, rows, HEAD_DIM), lambda b, j, pt: (b, 0, 0)),
                   pl.BlockSpec((None, keep, 128), lambda b, j, pt: (b, 0, 0))),
        scratch_shapes=[pltpu.VMEM((past + CMP_LEN, 128), F32), pltpu.VMEM((rows, n_selp), F32),
                        pltpu.VMEM((rows, n_selp), F32), pltpu.VMEM((n_selp, rows, HEAD_DIM), F32)],
    )
    return pl.pallas_call(
        kern,
        out_shape=(jax.ShapeDtypeStruct((n_seq, rows, HEAD_DIM), MXU_DTYPE),
                   jax.ShapeDtypeStruct((n_seq, keep, 128), F32)),
        grid_spec=grid_spec,
        compiler_params=_cparams("arbitrary", "arbitrary"),
        name="nsa_sample",
    )(page_table.reshape(-1), qa, *([cache] * pages_per_step), win_state, new_rows, ng8, posrows, w1p, w2p, wsel)


def sample_operands(dq, dkv, mq, mkv, nq, n4, nwin, ng, *, n_seq):
    per_seq = lambda a: a.astype(F32).reshape(n_seq, N_NEW, a.shape[-1])
    pad8 = lambda a: jnp.pad(a, [(0, 0)] * (a.ndim - 2) + [(0, 8 - N_NEW), (0, 0)])
    heads_first = lambda a, nh: a.reshape(n_seq, N_NEW, nh, a.shape[-1] // nh).transpose(0, 2, 1, 3)
    qd = heads_first(per_seq(dq), DIFF_HEADS)
    map_ok = jnp.asarray((np.arange(128) // HEAD_DIM)[None, :] == np.arange(2)[:, None], F32)
    diff_q = (qd[:, :, None, :, :] * map_ok[None, None, :, None, :]).reshape(n_seq, DIFF_HEADS, 8, 128)
    mkv_s = per_seq(mkv)
    qn = heads_first(per_seq(nq), NSA_HEADS).reshape(n_seq, N_NEW * NSA_HEADS, HEAD_DIM)
    g = per_seq(ng)
    return {
        "diff_q": diff_q.astype(MXU_DTYPE), "diff_new": pad8(per_seq(dkv)),
        "moba_q": pad8(heads_first(per_seq(mq), MOBA_HEADS)).astype(MXU_DTYPE),
        "moba_new_k": pad8(heads_first(mkv_s[..., 0:256], MOBA_HEADS)),
        "moba_new_v": pad8(heads_first(mkv_s[..., 256:512], MOBA_HEADS)),
        "nsa_qa": jnp.concatenate([qn, jnp.zeros_like(qn)], axis=-1).astype(MXU_DTYPE),
        "nsa_new": pad8(jnp.concatenate([per_seq(n4), per_seq(nwin)], axis=-1)),
        "nsa_g": jnp.concatenate([g, g], axis=1),
    }


def _tokens_first(o, n_seq, nh, rows_per_head):
    o = o.reshape(n_seq, nh, rows_per_head, o.shape[-1])[:, :, :N_NEW]
    return o.transpose(0, 2, 1, 3).reshape(n_seq * N_NEW, nh * o.shape[-1])


def kernel(x_prompt, x_sample, cache_diff_kv, cache_moba_kv, cache_nsa_kv, state_nsa_win, page_table, c_prompt, c_sample, w_ada, b_ada, g_mix, g_ffn, w_in, diff_lambda, diff_norm_g, cmp_pos, cmp_w1, cmp_w2, w_br_a, w_br_b, w_br_c, w_out, router_w, router_b, w_e_gate, w_e_up, w_e_down, g_final):
    batch, seq, d = x_prompt.shape
    n_seq, n_new, _ = x_sample.shape
    assert n_new == N_NEW and d == D_MODEL and seq % MOBA_BLOCK == 0
    depth = w_ada.shape[0]
    n_phys = cache_diff_kv.shape[1]
    past = page_table.shape[1] * PAGE_SIZE
    keep = state_nsa_win.shape[2]
    rows_s = n_seq * n_new

    n_cond = batch + n_seq
    c_all = jnp.concatenate([c_prompt, c_sample, jnp.zeros((-n_cond % 8, d), F32)], axis=0)
    mod = ada_ln(c_all, w_ada, b_ada)
    w1 = jnp.pad(w_in[:, :, :PROJ_COLS], ((0, 0), (0, 0), (0, PROJ_PAD - PROJ_COLS))).astype(MXU_DTYPE)
    wbg = w_in[:, :, PROJ_COLS:].astype(MXU_DTYPE)
    wa, wb, wo = w_br_a.astype(MXU_DTYPE), w_br_b.astype(MXU_DTYPE), w_out.astype(MXU_DTYPE)
    wc4 = w_br_c.reshape(depth, NSA_HEADS, HEAD_DIM, d)
    wc_wide = jnp.concatenate([jnp.zeros_like(wc4), wc4], axis=2).reshape(depth, NSA_HEADS * 128, d).astype(MXU_DTYPE)
    wc = w_br_c.astype(MXU_DTYPE)
    rw = jnp.pad(router_w, ((0, 0), (0, LANES - N_EXPERTS)))
    rb = jnp.pad(router_b, (0, LANES - N_EXPERTS)).reshape(1, LANES)
    wg, wu, wd = w_e_gate.astype(MXU_DTYPE), w_e_up.astype(MXU_DTYPE), w_e_down.astype(MXU_DTYPE)
    posrows, w1p, w2p = compress_weights(cmp_pos, cmp_w1, cmp_w2)
    g_mix3, g_ffn3 = g_mix.reshape(depth, 1, d), g_ffn.reshape(depth, 1, d)
    gf = g_final.reshape(1, d)
    norm_g_row = diff_norm_g.reshape(depth, 1, DIFF_VDIM)
    norm_g_col = diff_norm_g.reshape(depth, DIFF_VDIM, 1)
    tables_p = rope_tables(jnp.arange(seq, dtype=jnp.int32))
    tables_s = rope_tables(jnp.tile(past + jnp.arange(n_new, dtype=jnp.int32), n_seq))
    cache_d = cache_diff_kv.reshape(depth, n_phys, PAGE_SIZE * 2 * DIFF_HEADS, DIFF_VDIM)
    cache_m = jnp.transpose(cache_moba_kv, (0, 1, 3, 4, 5, 2))
    cache_n = jnp.transpose(cache_nsa_kv.reshape(depth, n_phys, PAGE_SIZE, 4, HEAD_DIM), (0, 1, 3, 4, 2))
    win_state = state_nsa_win.reshape(depth, n_seq, keep, 2 * HEAD_DIM)

    xp = x_prompt.reshape(batch * seq, d)
    xs = x_sample.reshape(rows_s, d)
    prompt = dict(rows_per_batch=seq, per_token=False)
    sample = dict(tm=rows_s, rows_per_batch=rows_s, per_token=True)
    leaves = [[] for _ in range(8)]
    for l in range(depth):
        last = l == depth - 1
        modp = mod[l, :batch].reshape(batch, 1, 6 * d)
        dq, dkv, mq, mkv, nq, n4, nwin, ng = project(xp, l, g_mix3, modp, modp, w1, tables_p, tm=256, **prompt)
        oa = diff_prompt(dq, dkv, diff_lambda, norm_g_col, l, batch=batch, seq=seq)
        ob = moba_prompt(mq, mkv, batch=batch, seq=seq)
        kvc = compress_prompt(n4, posrows, w1p, w2p, l, batch=batch, seq=seq)
        oc = nsa_prompt(nq, kvc, n4, nwin, ng, batch=batch, seq=seq)
        xp = merge(xp, oa, ob, oc, l, g_mix3, modp, wbg, wa, wb, wc_wide, wo, tm=512, **prompt)
        xp = moe(xp, l, g_ffn3, modp, rw, rb, wg, wu, wd, gf, tm=1024, final=last, **prompt)
        leaves[0].append(dkv.reshape(batch, seq, 2, DIFF_HEADS, DIFF_VDIM))
        leaves[2].append(mkv.reshape(batch, seq, 2, MOBA_HEADS, HEAD_DIM))
        leaves[4].append(n4.reshape(batch, seq, 4, 1, HEAD_DIM))
        leaves[6].append(nwin.reshape(batch, seq, 2, 1, HEAD_DIM)[:, seq - min(WINDOW, seq):])
        mods = jnp.repeat(mod[l, batch:n_cond], n_new, axis=0)
        dq, dkv, mq, mkv, nq, n4, nwin, ng = project(xs, l, g_mix3, mods, mods, w1, tables_s, **sample)
        ops = sample_operands(dq, dkv, mq, mkv, nq, n4, nwin, ng, n_seq=n_seq)
        oa = diff_sample(page_table, ops["diff_q"], cache_d, ops["diff_new"], diff_lambda, norm_g_row, l)
        ob = moba_sample(page_table, ops["moba_q"], cache_m, ops["moba_new_k"], ops["moba_new_v"], l)
        oc, new_win = nsa_sample(page_table, ops["nsa_qa"], cache_n, win_state, ops["nsa_new"], ops["nsa_g"],
                                 posrows, w1p, w2p, l)
        oa = oa[:, :n_new].reshape(rows_s, DIFF_HEADS * DIFF_VDIM)
        ob = _tokens_first(ob, n_seq, MOBA_HEADS, 8)
        oc = _tokens_first(oc, n_seq, NSA_HEADS, N_NEW)
        xs = merge(xs, oa, ob, oc, l, g_mix3, mods, wbg, wa, wb, wc, wo, **sample)
        xs = moe(xs, l, g_ffn3, mods, rw, rb, wg, wu, wd, gf, final=last, **sample)
        leaves[1].append(dkv.reshape(n_seq, n_new, 2, DIFF_HEADS, DIFF_VDIM))
        leaves[3].append(mkv.reshape(n_seq, n_new, 2, MOBA_HEADS, HEAD_DIM))
        leaves[5].append(n4.reshape(n_seq, n_new, 4, 1, HEAD_DIM))
        leaves[7].append(new_win.reshape(n_seq, keep, 2, 1, HEAD_DIM))
    return (xp.reshape(batch, seq, d), xs.reshape(n_seq, n_new, d)) + tuple(jnp.stack(v) for v in leaves)
```

```python
import functools
import math

import jax
import jax.numpy as jnp
import numpy as np
from jax import lax
from jax.experimental import pallas as pl
from jax.experimental.pallas import tpu as pltpu

D_MODEL = 1024
HEAD_DIM = 64
HALF = HEAD_DIM // 2
ROPE_THETA = 10000.0
RMS_EPS = 1e-6
PAGE_SIZE = 128
DIFF_HEADS = 4
DIFF_VDIM = 2 * HEAD_DIM
MOBA_HEADS = 4
MOBA_BLOCK = 256
MOBA_TOPK = 3
NSA_HEADS = 4
CMP_LEN = 32
CMP_STRIDE = 16
CMP_HIDDEN = 256
SEL_BLOCK = 64
SEL_TOPK = 16
WINDOW = 512
N_BRANCH = 3
N_NEW = 4
N_EXPERTS = 16
N_GROUPS = 4
EXPERTS_PER_GROUP = N_EXPERTS // N_GROUPS
EXPERT_FF = 512
PROJ_COLS = 2956
PROJ_PAD = 3072
Q_SCALE = HEAD_DIM ** -0.5

LANES = 128
VMEM_LIMIT_BYTES = 56 * 1024 * 1024

MXU_DTYPE = jnp.bfloat16
F32 = jnp.float32
NEG = -1e30


def _cparams(*sem):
    return pltpu.CompilerParams(dimension_semantics=sem, vmem_limit_bytes=VMEM_LIMIT_BYTES)


def _sigmoid(x):
    return 1.0 / (1.0 + jnp.exp(-x))


def _silu(x):
    return x * _sigmoid(x)


def _gelu_tanh(x):
    c = math.sqrt(2.0 / math.pi)
    return 0.5 * x * (1.0 + jnp.tanh(c * (x + 0.044715 * (x * x * x))))


def _dot(a, b):
    return jnp.dot(a.astype(MXU_DTYPE), b.astype(MXU_DTYPE), preferred_element_type=F32)


def _dot_nt(a, b):
    return lax.dot_general(a.astype(MXU_DTYPE), b.astype(MXU_DTYPE), (((1,), (1,)), ((), ())),
                           preferred_element_type=F32)


def _rms(x):
    return x * lax.rsqrt(jnp.mean(x * x, axis=-1, keepdims=True) + RMS_EPS)


def _mod_norm(x, g, shift, scale):
    return (_rms(x) * g) * (1.0 + scale) + shift


def _lane(shape, dim=None):
    return lax.broadcasted_iota(jnp.int32, shape, len(shape) - 1 if dim is None else dim)


def _rope(y, cos_t, sin_t):
    outs = []
    for c in range(y.shape[1] // LANES):
        blk = y[:, c * LANES:(c + 1) * LANES]
        first = (_lane(blk.shape) % HEAD_DIM) < HALF
        partner = jnp.where(first, pltpu.roll(blk, LANES - HALF, 1), pltpu.roll(blk, HALF, 1))
        outs.append(blk * cos_t + partner * sin_t)
    return outs[0] if len(outs) == 1 else jnp.concatenate(outs, axis=1)


def _flash_next(s, v_fn, m_ref, l_ref, acc_ref):
    m_old = m_ref[...]
    m = jnp.maximum(m_old, jnp.max(s, axis=-1, keepdims=True))
    alpha = jnp.exp(m_old - m)
    p = jnp.exp(s - m)
    m_ref[...] = m
    l_ref[...] = alpha * l_ref[...] + jnp.sum(p, axis=-1, keepdims=True)
    acc_ref[...] = alpha * acc_ref[...] + v_fn(p)


def _flash_t_next(s, vt, m_ref, l_ref, acc_ref):
    m_old = m_ref[...]
    m = jnp.maximum(m_old, jnp.max(s, axis=0, keepdims=True))
    alpha = jnp.exp(m_old - m)
    p = jnp.exp(s - m)
    m_ref[...] = m
    l_ref[...] = alpha * l_ref[...] + jnp.sum(p, axis=0, keepdims=True)
    acc_ref[...] = alpha * acc_ref[...] + _dot(vt, p)


def _flash_t_loop(score_fn, vt_ref, stats, sbuf, first_blk, lo, hi):
    for m_ref, l_ref, acc_ref in stats:
        m_ref[...] = jnp.full(m_ref.shape, NEG, F32)
        l_ref[...] = jnp.zeros(l_ref.shape, F32)
        acc_ref[...] = jnp.zeros(acc_ref.shape, F32)

    if sbuf is None:
        def direct(blk, first):
            vt = vt_ref[blk]
            for s, (m_ref, l_ref, acc_ref) in zip(score_fn(blk, first), stats):
                _flash_t_next(s, vt, m_ref, l_ref, acc_ref)

        direct(first_blk, True)

        def body(t, carry):
            direct(t, False)
            return carry

        lax.fori_loop(lo, hi, body, 0)
        return

    for mi, s in enumerate(score_fn(first_blk, True)):
        sbuf[0, mi] = s

    def consume(slot, blk):
        vt = vt_ref[blk]
        for mi, (m_ref, l_ref, acc_ref) in enumerate(stats):
            _flash_t_next(sbuf[slot, mi], vt, m_ref, l_ref, acc_ref)

    def body(t, carry):
        k = t - lo
        nxt = score_fn(t, False)
        consume(k & 1, jnp.where(k == 0, first_blk, t - 1))
        for mi, s in enumerate(nxt):
            sbuf[1 - (k & 1), mi] = s
        return carry

    lax.fori_loop(lo, hi, body, 0)
    n = hi - lo
    consume(n & 1, jnp.where(n == 0, first_blk, hi - 1))


def _top_k_rows(vals, k):
    row = _lane(vals.shape, 0)
    big = jnp.int32(vals.shape[0])

    def body(_, carry):
        sel, cur = carry
        m = jnp.max(cur, axis=0, keepdims=True)
        idx = jnp.min(jnp.where(cur == m, row, big), axis=0, keepdims=True)
        hit = row == idx
        return jnp.where(hit, 1.0, sel), jnp.where(hit, -jnp.inf, cur)

    sel, _ = lax.fori_loop(0, k, body, (jnp.zeros(vals.shape, F32), vals))
    return sel


def _top_k_mask(vals, k, n_valid_lanes):
    lane = _lane(vals.shape)
    big = jnp.int32(vals.shape[-1])

    def body(_, carry):
        sel, cur = carry
        m = jnp.max(cur, axis=-1, keepdims=True)
        idx = jnp.min(jnp.where(cur == m, lane, big), axis=-1, keepdims=True)
        hit = lane == idx
        return jnp.where(hit, 1.0, sel), jnp.where(hit, -jnp.inf, cur)

    cur0 = jnp.where(lane < n_valid_lanes, vals, -jnp.inf)
    sel, _ = lax.fori_loop(0, k, body, (jnp.zeros(vals.shape, F32), cur0))
    return sel


def _ada_kernel(c_ref, w_ref, b_ref, o_ref):
    o_ref[0] = _dot(_silu(c_ref[...]), w_ref[0]) + b_ref[0]


def ada_ln(c_all, w_ada, b_ada):
    depth, _, n = w_ada.shape
    rc = c_all.shape[0]
    tn = 1536
    return pl.pallas_call(
        _ada_kernel,
        out_shape=jax.ShapeDtypeStruct((depth, rc, n), F32),
        grid=(depth, n // tn),
        in_specs=[pl.BlockSpec((rc, D_MODEL), lambda l, j: (0, 0)),
                  pl.BlockSpec((1, D_MODEL, tn), lambda l, j: (l, 0, j)),
                  pl.BlockSpec((1, 1, tn), lambda l, j: (l, 0, j))],
        out_specs=pl.BlockSpec((1, rc, tn), lambda l, j: (l, 0, j)),
        compiler_params=_cparams("arbitrary", "arbitrary"),
        name="ada_ln",
    )(c_all, w_ada, b_ada.reshape(depth, 1, n))


def _mod_spec(per_token, tm, rows_per_batch_tile, chunk):
    if per_token:
        return pl.BlockSpec((tm, D_MODEL), lambda i: (i, chunk))
    return pl.BlockSpec((None, 1, D_MODEL), lambda i: (i // rows_per_batch_tile, 0, chunk))


def _proj_kernel(x_ref, g_ref, sh_ref, sc_ref, w_ref, cf_ref, sf_ref, ck_ref, sk_ref,
                 dq_ref, dkv_ref, mq_ref, mkv_ref, nq_ref, n4_ref, nwin_ref, ng_ref):
    h = _mod_norm(x_ref[...], g_ref[0], sh_ref[...], sc_ref[...])
    y = _dot(h, w_ref[0])
    cf, sf, ck, sk = cf_ref[...], sf_ref[...], ck_ref[...], sk_ref[...]
    dq_ref[...] = (_rope(y[:, 0:512], cf, sf) * Q_SCALE).astype(dq_ref.dtype)
    dkv_ref[:, 0:512] = _rope(y[:, 512:1024], cf, sf)
    dkv_ref[:, 512:1024] = y[:, 1024:1536]
    mq_ref[...] = (_rope(y[:, 1536:1792], cf, sf) * Q_SCALE).astype(mq_ref.dtype)
    mkv_ref[:, 0:256] = _rope(y[:, 1792:2048], cf, sf)
    mkv_ref[:, 256:512] = y[:, 2048:2304]
    nq_ref[...] = _rope(y[:, 2304:2560], cf, sf) * Q_SCALE
    n4_ref[...] = _rope(y[:, 2560:2816], ck, sk)
    nwin_ref[...] = _rope(y[:, 2816:2944], ck, sk)
    ng_ref[...] = _sigmoid(y[:, 2944:3072])


def project(x, layer, g_mix, shift, scale, w1, tables, *, tm, rows_per_batch, per_token):
    r = x.shape[0]
    nt = rows_per_batch // tm
    row = lambda w: pl.BlockSpec((tm, w), lambda i: (i, 0))
    tab = pl.BlockSpec((tm, LANES), lambda i: (i % nt, 0))
    out_shapes = (
        jax.ShapeDtypeStruct((r, 512), MXU_DTYPE),
        jax.ShapeDtypeStruct((r, 1024), F32),
        jax.ShapeDtypeStruct((r, 256), MXU_DTYPE),
        jax.ShapeDtypeStruct((r, 512), F32),
        jax.ShapeDtypeStruct((r, 256), F32),
        jax.ShapeDtypeStruct((r, 256), F32),
        jax.ShapeDtypeStruct((r, 128), F32),
        jax.ShapeDtypeStruct((r, 128), F32),
    )
    return pl.pallas_call(
        _proj_kernel,
        out_shape=out_shapes,
        grid=(r // tm,),
        in_specs=[row(D_MODEL),
                  pl.BlockSpec((1, 1, D_MODEL), lambda i: (layer, 0, 0)),
                  _mod_spec(per_token, tm, nt, 0), _mod_spec(per_token, tm, nt, 1),
                  pl.BlockSpec((1, D_MODEL, PROJ_PAD), lambda i: (layer, 0, 0)),
                  tab, tab, tab, tab],
        out_specs=tuple(row(s.shape[1]) for s in out_shapes),
        compiler_params=_cparams("arbitrary"),
        name="project",
    )(x, g_mix, shift, scale, w1, *tables)


def rope_tables(pos):
    freqs = ROPE_THETA ** (-jnp.arange(HALF, dtype=F32) / HALF)
    ang = pos.astype(F32)[:, None] * freqs[None, :]
    cos, sin = jnp.cos(ang), jnp.sin(ang)
    c64 = jnp.concatenate([cos, cos], axis=1)
    s64 = jnp.concatenate([-sin, sin], axis=1)
    one, zero = jnp.ones_like(c64), jnp.zeros_like(s64)
    return (jnp.concatenate([c64, c64], axis=1), jnp.concatenate([s64, s64], axis=1),
            jnp.concatenate([c64, one], axis=1), jnp.concatenate([s64, zero], axis=1))


def _diff_lambda(lam_ref, lam_init):
    lp = lam_ref[0]
    a = jnp.sum(lp[0:1] * lp[1:2], axis=-1, keepdims=True)
    b = jnp.sum(lp[2:3] * lp[3:4], axis=-1, keepdims=True)
    return jnp.exp(a) - jnp.exp(b) + lam_init


def _fill_kv(k_ref, v_ref, kb, vt):
    kb[...] = k_ref[...].astype(kb.dtype)
    tk = vt.shape[2]
    for c in range(vt.shape[0]):
        vt[c] = v_ref[c * tk:(c + 1) * tk, :].T.astype(vt.dtype)


def _split_heads_t(q_ref):
    qt = q_ref[...].astype(F32).T
    top = _lane(qt.shape, 0) < HEAD_DIM
    return jnp.where(top, qt, 0.0).astype(MXU_DTYPE), jnp.where(top, 0.0, qt).astype(MXU_DTYPE)


def _diff_prompt_kernel(q_ref, k_ref, v_ref, lam_ref, g_ref, o_ref,
                        kb, vt, m1, l1, a1, m2, l2, a2, *, tq, tk, lam_init):
    i = pl.program_id(2)

    @pl.when(i == 0)
    def _():
        _fill_kv(k_ref, v_ref, kb, vt)

    qs = _split_heads_t(q_ref)
    q0 = i * tq
    jd = q0 // tk

    def scores(j, diag):
        j0 = pl.multiple_of(j * tk, tk)
        kj = kb[pl.ds(j0, tk), :]
        out = [_dot(kj, qm) for qm in qs]
        if diag:
            causal = j0 + _lane((tk, tq), 0) <= q0 + _lane((tk, tq), 1)
            out = [jnp.where(causal, s, NEG) for s in out]
        return out

    _flash_t_loop(scores, vt, ((m1, l1, a1), (m2, l2, a2)), None, jd, 0, jd)

    lam = _diff_lambda(lam_ref, lam_init)
    o = a1[...] / l1[...] - lam * (a2[...] / l2[...])
    o = o * lax.rsqrt(jnp.mean(o * o, axis=0, keepdims=True) + RMS_EPS) * g_ref[0] * (1.0 - lam_init)
    o_ref[...] = o.T.astype(o_ref.dtype)


def diff_prompt(dq, dkv, diff_lambda, diff_norm_g, layer, *, batch, seq, tq=512, tk=512):
    nq = seq // tq
    lam_init = 0.8 - 0.6 * math.exp(-0.3 * layer)
    kern = functools.partial(_diff_prompt_kernel, tq=tq, tk=tk, lam_init=lam_init)
    stat = pltpu.VMEM((1, tq), F32)
    acc = pltpu.VMEM((DIFF_VDIM, tq), F32)
    return pl.pallas_call(
        kern,
        out_shape=jax.ShapeDtypeStruct((batch * seq, DIFF_HEADS * DIFF_VDIM), MXU_DTYPE),
        grid=(batch, DIFF_HEADS, nq),
        in_specs=[pl.BlockSpec((tq, 128), lambda b, h, i: (b * nq + i, h)),
                  pl.BlockSpec((seq, 128), lambda b, h, i: (b, h)),
                  pl.BlockSpec((seq, 128), lambda b, h, i: (b, DIFF_HEADS + h)),
                  pl.BlockSpec((1, 4, HEAD_DIM), lambda b, h, i: (layer, 0, 0)),
                  pl.BlockSpec((1, DIFF_VDIM, 1), lambda b, h, i: (layer, 0, 0))],
        out_specs=pl.BlockSpec((tq, 128), lambda b, h, i: (b * nq + i, h)),
        scratch_shapes=[pltpu.VMEM((seq, 128), MXU_DTYPE), pltpu.VMEM((seq // tk, 128, tk), MXU_DTYPE),
                        stat, stat, acc, stat, stat, acc],
        compiler_params=_cparams("arbitrary", "arbitrary", "arbitrary"),
        name="diff_prompt",
    )(dq, dkv, dkv, diff_lambda, diff_norm_g)


def _moba_prompt_kernel(q_ref, k_ref, v_ref, o_ref, kb, vt, kmean, sel1, sel2, sbuf, m1, l1, a1, m2, l2, a2,
                        *, n_blk, tk):
    i = pl.program_id(2)
    blk = MOBA_BLOCK
    per = tk // blk

    @pl.when(i == 0)
    def _():
        _fill_kv(k_ref, v_ref, kb, vt)
        for n in range(n_blk):
            kmean[n:n + 1, :] = jnp.sum(k_ref[n * blk:(n + 1) * blk, :], axis=0, keepdims=True) * (1.0 / blk)

    qs = _split_heads_t(q_ref)
    sels = (sel1, sel2)
    km = kmean[...]
    brow = _lane((n_blk, blk), 0)
    for qm, sel_ref in zip(qs, sels):
        gate = jnp.where(brow < i, _dot(km, qm), -jnp.inf)
        sel_ref[...] = jnp.where(brow < i, _top_k_rows(gate, MOBA_TOPK), 0.0)

    causal = jnp.where(_lane((blk, blk), 0) <= _lane((blk, blk), 1), 1.0, 0.0)

    def scores(j, _):
        kj = kb[pl.ds(pl.multiple_of(j * tk, tk), tk), :]
        raw = [_dot(kj, qm) for qm in qs]
        out = []
        for s, sel_ref in zip(raw, sels):
            on = []
            for u in range(per):
                n = j * per + u
                past = jnp.broadcast_to(sel_ref[pl.ds(n, 1), :], (blk, blk))
                on.append(jnp.where(n == i, causal, jnp.where(n < i, past, 0.0)))
            out.append(jnp.where(jnp.concatenate(on, axis=0) > 0.5, s, NEG))
        return out

    jd = i // per
    _flash_t_loop(scores, vt, ((m1, l1, a1), (m2, l2, a2)), sbuf, jd, 0, jd)
    o = jnp.where(_lane((128, blk), 0) < HEAD_DIM, a1[...] / l1[...], a2[...] / l2[...])
    o_ref[...] = o.T.astype(o_ref.dtype)


def moba_prompt(mq, mkv, *, batch, seq, tk=512):
    blk = MOBA_BLOCK
    nq = seq // blk
    assert seq % tk == 0
    kern = functools.partial(_moba_prompt_kernel, n_blk=nq, tk=tk)
    stat = pltpu.VMEM((1, blk), F32)
    acc = pltpu.VMEM((128, blk), F32)
    sel = pltpu.VMEM((nq, blk), F32)
    pairs = MOBA_HEADS // 2
    return pl.pallas_call(
        kern,
        out_shape=jax.ShapeDtypeStruct((batch * seq, MOBA_HEADS * HEAD_DIM), MXU_DTYPE),
        grid=(batch, pairs, nq),
        in_specs=[pl.BlockSpec((blk, 128), lambda b, h, i: (b * nq + i, h)),
                  pl.BlockSpec((seq, 128), lambda b, h, i: (b, h)),
                  pl.BlockSpec((seq, 128), lambda b, h, i: (b, pairs + h))],
        out_specs=pl.BlockSpec((blk, 128), lambda b, h, i: (b * nq + i, h)),
        scratch_shapes=[pltpu.VMEM((seq, 128), MXU_DTYPE), pltpu.VMEM((seq // tk, 128, tk), MXU_DTYPE),
                        pltpu.VMEM((nq, 128), F32), sel, sel, pltpu.VMEM((2, 2, tk, blk), F32),
                        stat, stat, acc, stat, stat, acc],
        compiler_params=_cparams("arbitrary", "arbitrary", "arbitrary"),
        name="moba_prompt",
    )(mq, mkv, mkv)


def compress_weights(cmp_pos, cmp_w1, cmp_w2):
    depth = cmp_w1.shape[0]
    w1k = cmp_w1[:, 0].reshape(depth, CMP_LEN, HEAD_DIM, CMP_HIDDEN)
    w1v = cmp_w1[:, 1].reshape(depth, CMP_LEN, HEAD_DIM, CMP_HIDDEN)
    z = jnp.zeros_like(w1k)
    per_row = jnp.concatenate([jnp.concatenate([w1k, z], axis=-1), jnp.concatenate([z, w1v], axis=-1)], axis=2)
    w1p = per_row.reshape(depth, CMP_LEN // 2, 4 * HEAD_DIM, 2 * CMP_HIDDEN).astype(MXU_DTYPE)
    posrows = jnp.transpose(cmp_pos, (0, 2, 1, 3)).reshape(depth, CMP_LEN, 1, 2 * HEAD_DIM)
    z2 = jnp.zeros_like(cmp_w2[:, 0])
    w2p = jnp.concatenate([jnp.concatenate([cmp_w2[:, 0], z2], axis=-1),
                           jnp.concatenate([z2, cmp_w2[:, 1]], axis=-1)], axis=1).astype(MXU_DTYPE)
    return posrows, w1p, w2p


def _compress_rows(xs, n_tok, pos_ref, w1_ref, w2_ref):
    acc = jnp.zeros((n_tok, 2 * CMP_HIDDEN), F32)
    for p in range(CMP_LEN // 2):
        xa = xs[pl.ds(2 * p, n_tok, stride=CMP_STRIDE), :] + pos_ref[0, 2 * p]
        xb = xs[pl.ds(2 * p + 1, n_tok, stride=CMP_STRIDE), :] + pos_ref[0, 2 * p + 1]
        acc = acc + _dot(jnp.concatenate([xa, xb], axis=1), w1_ref[0, p])
    return _dot(_gelu_tanh(acc), w2_ref[0])


def _compress_prompt_kernel(x_ref, pos_ref, w1_ref, w2_ref, o_ref, xs, *, seq, n_tok):
    xs[0:seq, :] = x_ref[...]
    xs[seq:seq + CMP_LEN, :] = jnp.zeros((CMP_LEN, 128), F32)
    o_ref[...] = _compress_rows(xs, n_tok, pos_ref, w1_ref, w2_ref)


def compress_prompt(n4, posrows, w1p, w2p, layer, *, batch, seq):
    n_tok = seq // CMP_STRIDE
    kern = functools.partial(_compress_prompt_kernel, seq=seq, n_tok=n_tok)
    return pl.pallas_call(
        kern,
        out_shape=jax.ShapeDtypeStruct((batch, n_tok, 128), F32),
        grid=(batch,),
        in_specs=[pl.BlockSpec((seq, 128), lambda b: (b, 0)),
                  pl.BlockSpec((1, CMP_LEN, 1, 128), lambda b: (layer, 0, 0, 0)),
                  pl.BlockSpec((1, CMP_LEN // 2, 256, 512), lambda b: (layer, 0, 0, 0)),
                  pl.BlockSpec((1, 512, 128), lambda b: (layer, 0, 0))],
        out_specs=pl.BlockSpec((None, n_tok, 128), lambda b: (b, 0, 0)),
        scratch_shapes=[pltpu.VMEM((seq + CMP_LEN, 128), F32)],
        compiler_params=_cparams("arbitrary"),
        name="compress_prompt",
    )(n4, posrows, w1p, w2p)


def _cmp_to_sel_weights(n_cmp, n_cmp_pad, n_sel_pad):
    ratio_sel = SEL_BLOCK // CMP_STRIDE
    ratio_cmp = CMP_LEN // CMP_STRIDE
    off = np.arange(n_cmp)[:, None] - ratio_sel * np.arange(n_sel_pad)[None, :]
    shifts = (np.arange(ratio_sel)[:, None] - np.arange(ratio_cmp)[None, :]).reshape(-1)
    w = np.sum(off[..., None] == shifts, axis=-1).astype(np.float32)
    return np.pad(w, ((0, n_cmp_pad - n_cmp), (0, 0)))


def _nsa_prompt_kernel(q_ref, kvc_ref, sel_ref, win_ref, ng_ref, wselt_ref, o_ref,
                       kvc_b, kvc_t, sel_b, sel_t, win_b, win_t, selm, sbuf, m_s, l_s, a_s, m_w, l_w, a_w,
                       *, tq, tk, n_cmp):
    i = pl.program_id(1)
    nh = NSA_HEADS

    @pl.when(i == 0)
    def _():
        kvc = kvc_ref[...]
        kvc_b[...] = kvc.astype(kvc_b.dtype)
        kvc_t[...] = kvc.T.astype(kvc_t.dtype)
        _fill_kv(sel_ref, sel_ref, sel_b, sel_t)
        _fill_kv(win_ref, win_ref, win_b, win_t)

    qt = q_ref[...].T
    zero = jnp.zeros((HEAD_DIM, tq), F32)
    qp = jnp.concatenate([jnp.concatenate([qt[HEAD_DIM * h:HEAD_DIM * (h + 1)], zero], axis=0) for h in range(nh)],
                         axis=1).astype(MXU_DTYPE)
    cols = nh * tq
    q0 = i * tq
    pos = q0 + (_lane((1, cols)) & (tq - 1))
    pos_q = q0 + _lane((1, tq))

    s = _dot(kvc_b[...], qp)
    n = _lane(s.shape, 0)
    vis = jnp.where(n * CMP_STRIDE + (CMP_LEN - 1) <= pos, n, n_cmp) < n_cmp
    s = jnp.where(vis, s, NEG)
    e = jnp.where(vis, jnp.exp(s - jnp.max(s, axis=0, keepdims=True)), 0.0)
    den = jnp.sum(e, axis=0, keepdims=True)
    p = (e / jnp.where(den > 0, den, 1.0)).astype(MXU_DTYPE)
    o_cmp = _dot(kvc_t[...], p)
    wselt = wselt_ref[...]
    imp = _dot(wselt, p[:, 0:tq])
    for h in range(1, nh):
        imp = imp + _dot(wselt, p[:, h * tq:(h + 1) * tq])

    own = lax.shift_right_logical(pos_q, int(math.log2(SEL_BLOCK)))
    jb = _lane(imp.shape, 0)
    forced = jnp.where(jb == 0, 1, jnp.where(jb == own, 1, jnp.where(jb == own - 1, 1, 0))) > 0
    impm = jnp.where(jb <= own, jnp.where(forced, jnp.inf, imp), -jnp.inf)
    selm[...] = jnp.where(jb <= own, _top_k_rows(impm, SEL_TOPK), 0.0)

    jd = q0 // tk
    per = tk // SEL_BLOCK

    def sel_scores(j, diag):
        j0 = pl.multiple_of(j * tk, tk)
        s = _dot(sel_b[pl.ds(j0, tk), :], qp)
        on = jnp.concatenate([jnp.broadcast_to(selm[pl.ds(j * per + u, 1), :], (SEL_BLOCK, tq)) for u in range(per)],
                             axis=0)
        on = jnp.concatenate([on] * nh, axis=1)
        if diag:
            on = jnp.where(j0 + _lane((tk, cols), 0) <= pos, on, 0.0)
        return [jnp.where(on > 0.5, s, NEG)]

    _flash_t_loop(sel_scores, sel_t, ((m_s, l_s, a_s),), sbuf, jd, 0, jd)

    def win_scores(j, _):
        j0 = pl.multiple_of(j * tk, tk)
        s = _dot(win_b[pl.ds(j0, tk), :], qp)
        key = j0 + _lane((tk, cols), 0)
        on = jnp.where(key <= pos, jnp.where(key > pos - WINDOW, 1, 0), 0) > 0
        return [jnp.where(on, s, NEG)]

    _flash_t_loop(win_scores, win_t, ((m_w, l_w, a_w),), sbuf, jd, jnp.maximum(q0 - (WINDOW - 1), 0) // tk, jd)

    n_pad = jnp.maximum(WINDOW - 1 - pos, 0).astype(F32)
    m_win = m_w[...]
    m_all = jnp.where(n_pad > 0, jnp.maximum(m_win, 0.0), m_win)
    shrink = jnp.exp(m_win - m_all)
    o_win = (a_w[...] * shrink) / (l_w[...] * shrink + n_pad * jnp.exp(-m_all))

    gt = ng_ref[...].T

    def gate(c):
        return jnp.concatenate([gt[3 * h + c:3 * h + c + 1, :] for h in range(nh)], axis=1)

    o = gate(0) * o_cmp + gate(1) * (a_s[...] / l_s[...]) + gate(2) * o_win
    o = jnp.where(_lane(o.shape, 0) >= HEAD_DIM, o, 0.0)
    o_ref[...] = jnp.concatenate([o[:, h * tq:(h + 1) * tq].T for h in range(nh)], axis=1).astype(o_ref.dtype)


def nsa_prompt(nq, kvc, n4, nwin, ng, *, batch, seq, tq=256, tk=512):
    nqb = seq // tq
    n_cmp = (seq - CMP_LEN) // CMP_STRIDE + 1
    n_cmp_pad = kvc.shape[1]
    wselt = jnp.asarray(_cmp_to_sel_weights(n_cmp, n_cmp_pad, 128).T, MXU_DTYPE)
    cols = NSA_HEADS * tq
    kern = functools.partial(_nsa_prompt_kernel, tq=tq, tk=tk, n_cmp=n_cmp)
    stat = pltpu.VMEM((1, cols), F32)
    acc = pltpu.VMEM((128, cols), F32)
    rows = pltpu.VMEM((seq, 128), MXU_DTYPE)
    trans = pltpu.VMEM((seq // tk, 128, tk), MXU_DTYPE)
    return pl.pallas_call(
        kern,
        out_shape=jax.ShapeDtypeStruct((batch * seq, NSA_HEADS * 128), MXU_DTYPE),
        grid=(batch, nqb),
        in_specs=[pl.BlockSpec((tq, 256), lambda b, i: (b * nqb + i, 0)),
                  pl.BlockSpec((None, n_cmp_pad, 128), lambda b, i: (b, 0, 0)),
                  pl.BlockSpec((seq, 128), lambda b, i: (b, 1)),
                  pl.BlockSpec((seq, 128), lambda b, i: (b, 0)),
                  pl.BlockSpec((tq, 128), lambda b, i: (b * nqb + i, 0)),
                  pl.BlockSpec((128, n_cmp_pad), lambda b, i: (0, 0))],
        out_specs=pl.BlockSpec((tq, NSA_HEADS * 128), lambda b, i: (b * nqb + i, 0)),
        scratch_shapes=[pltpu.VMEM((n_cmp_pad, 128), MXU_DTYPE), pltpu.VMEM((128, n_cmp_pad), MXU_DTYPE),
                        rows, trans, rows, trans, pltpu.VMEM((128, tq), F32), pltpu.VMEM((2, 1, tk, cols), F32),
                        stat, stat, acc, stat, stat, acc],
        compiler_params=_cparams("arbitrary", "arbitrary"),
        name="nsa_prompt",
    )(nq, kvc, n4, nwin, ng, wselt)


def _merge_kernel(x_ref, oa_ref, ob_ref, oc_ref, g_ref, sh_ref, sc_ref, gt_ref,
                  wbg_ref, wa_ref, wb_ref, wc_ref, wo_ref, o_ref):
    x = x_ref[...]
    h = _mod_norm(x, g_ref[0], sh_ref[...], sc_ref[...])
    bg = _sigmoid(_dot(h, wbg_ref[0]))
    d = D_MODEL
    m = (bg[:, 0:d] * _dot(oa_ref[...], wa_ref[0]) + bg[:, d:2 * d] * _dot(ob_ref[...], wb_ref[0])
         + bg[:, 2 * d:3 * d] * _dot(oc_ref[...], wc_ref[0]))
    o_ref[...] = x + gt_ref[...] * _dot(m, wo_ref[0])


def merge(x, oa, ob, oc, layer, g_mix, mod, wbg, wa, wb, wc, wo, *, tm, rows_per_batch, per_token):
    r = x.shape[0]
    nt = rows_per_batch // tm
    row = lambda w: pl.BlockSpec((tm, w), lambda i: (i, 0))
    wspec = lambda a: pl.BlockSpec((1,) + a.shape[1:], lambda i: (layer, 0, 0))
    return pl.pallas_call(
        _merge_kernel,
        out_shape=jax.ShapeDtypeStruct((r, D_MODEL), F32),
        grid=(r // tm,),
        in_specs=[row(D_MODEL), row(oa.shape[1]), row(ob.shape[1]), row(oc.shape[1]),
                  pl.BlockSpec((1, 1, D_MODEL), lambda i: (layer, 0, 0)),
                  _mod_spec(per_token, tm, nt, 0), _mod_spec(per_token, tm, nt, 1), _mod_spec(per_token, tm, nt, 2),
                  wspec(wbg), wspec(wa), wspec(wb), wspec(wc), wspec(wo)],
        out_specs=row(D_MODEL),
        compiler_params=_cparams("arbitrary"),
        name="merge",
    )(x, oa, ob, oc, g_mix, mod, mod, mod, wbg, wa, wb, wc, wo)


def _route(h, rw_ref, rb_ref):
    logits = jnp.dot(h, rw_ref[...], precision=lax.Precision.HIGHEST, preferred_element_type=F32)
    aff = _sigmoid(logits)
    biased = aff + rb_ref[...]
    lane = _lane(aff.shape)
    grp = lax.shift_right_logical(lane, int(math.log2(EXPERTS_PER_GROUP)))
    sentinel = jnp.int32(aff.shape[-1])
    best = None
    for gi in range(N_GROUPS):
        v1 = jnp.where(grp == gi, biased, -jnp.inf)
        m1 = jnp.max(v1, axis=-1, keepdims=True)
        i1 = jnp.min(jnp.where(v1 == m1, lane, sentinel), axis=-1, keepdims=True)
        v2 = jnp.where(lane == i1, -jnp.inf, v1)
        m2 = jnp.max(v2, axis=-1, keepdims=True)
        i2 = jnp.min(jnp.where(v2 == m2, lane, sentinel), axis=-1, keepdims=True)
        cand = (m1 + m2, i1, i2)
        if best is None:
            best = cand
        else:
            better = cand[0] > best[0]
            best = tuple(jnp.where(better, c, b) for c, b in zip(cand, best))
    _, i1, i2 = best
    w1 = jnp.sum(jnp.where(lane == i1, aff, 0.0), axis=-1, keepdims=True)
    w2 = jnp.sum(jnp.where(lane == i2, aff, 0.0), axis=-1, keepdims=True)
    tot = w1 + w2
    return jnp.where(lane == i1, w1 / tot, 0.0) + jnp.where(lane == i2, w2 / tot, 0.0)


def _moe_kernel(x_ref, g_ref, sh_ref, sc_ref, gt_ref, rw_ref, rb_ref, wg_ref, wu_ref, wd_ref, gf_ref,
                o_ref, hb, gate, acc, *, final):
    e = pl.program_id(1)

    @pl.when(e == 0)
    def _():
        h = _mod_norm(x_ref[...], g_ref[0], sh_ref[...], sc_ref[...])
        hb[...] = h.astype(hb.dtype)
        gate[...] = _route(h, rw_ref, rb_ref)
        acc[...] = jnp.zeros_like(acc)

    h = hb[...]
    gv = gate[...]
    ge = jnp.sum(jnp.where(_lane(gv.shape) == e, gv, 0.0), axis=-1, keepdims=True)
    hid = _silu(_dot(h, wg_ref[0, 0])) * _dot(h, wu_ref[0, 0]) * ge
    acc[...] += _dot(hid, wd_ref[0, 0])

    @pl.when(e == N_EXPERTS - 1)
    def _():
        y = x_ref[...] + gt_ref[...] * acc[...]
        if final:
            y = _rms(y) * gf_ref[...]
        o_ref[...] = y


def moe(x, layer, g_ffn, mod, rw, rb, wg, wu, wd, g_final, *, tm, rows_per_batch, per_token, final):
    r = x.shape[0]
    nt = rows_per_batch // tm
    row = pl.BlockSpec((tm, D_MODEL), lambda i, e: (i, 0))
    if per_token:
        mspec = lambda c: pl.BlockSpec((tm, D_MODEL), lambda i, e: (i, c))
    else:
        mspec = lambda c: pl.BlockSpec((None, 1, D_MODEL), lambda i, e: (i // nt, 0, c))
    kern = functools.partial(_moe_kernel, final=final)
    return pl.pallas_call(
        kern,
        out_shape=jax.ShapeDtypeStruct((r, D_MODEL), F32),
        grid=(r // tm, N_EXPERTS),
        in_specs=[row, pl.BlockSpec((1, 1, D_MODEL), lambda i, e: (layer, 0, 0)),
                  mspec(3), mspec(4), mspec(5),
                  pl.BlockSpec((D_MODEL, 128), lambda i, e: (0, 0)),
                  pl.BlockSpec((1, 128), lambda i, e: (0, 0)),
                  pl.BlockSpec((1, 1, D_MODEL, EXPERT_FF), lambda i, e: (layer, e, 0, 0)),
                  pl.BlockSpec((1, 1, D_MODEL, EXPERT_FF), lambda i, e: (layer, e, 0, 0)),
                  pl.BlockSpec((1, 1, EXPERT_FF, D_MODEL), lambda i, e: (layer, e, 0, 0)),
                  pl.BlockSpec((1, D_MODEL), lambda i, e: (0, 0))],
        out_specs=row,
        scratch_shapes=[pltpu.VMEM((tm, D_MODEL), MXU_DTYPE), pltpu.VMEM((tm, 128), F32),
                        pltpu.VMEM((tm, D_MODEL), F32)],
        compiler_params=_cparams("arbitrary", "arbitrary"),
        name="moe",
    )(x, g_ffn, mod, mod, mod, rw, rb, wg, wu, wd, g_final)


def _page_specs(n_pages, pages_per_step, layer, block):
    zeros = (0,) * len(block)

    def spec(p):
        return pl.BlockSpec((None, None) + block,
                            lambda b, j, pt: (layer, pt[b * n_pages + j * pages_per_step + p]) + zeros)
    return [spec(p) for p in range(pages_per_step)]


def _new_token_mask(rows):
    t = lax.broadcasted_iota(jnp.int32, (rows, 8), 0) & (N_NEW - 1)
    tk = _lane((rows, 8))
    return jnp.where(tk <= t, jnp.where(tk < N_NEW, 1, 0), 0) > 0


def _diff_sample_kernel(pt_ref, q_ref, *rest, n_pg, lam_init):
    pages = rest[:n_pg]
    new_ref, lam_ref, g_ref, o_ref, m_ref, l_ref, a_ref = rest[n_pg:]
    j = pl.program_id(1)
    nh = DIFF_HEADS

    @pl.when(j == 0)
    def _():
        m_ref[...] = jnp.full(m_ref.shape, NEG, F32)
        l_ref[...] = jnp.zeros(l_ref.shape, F32)
        a_ref[...] = jnp.zeros(a_ref.shape, F32)

    qh = [q_ref[h] for h in range(nh)]

    def keys(pg, h):
        return pg[pl.ds(h, PAGE_SIZE, stride=2 * nh), :].astype(MXU_DTYPE)

    def values(pg, h):
        return pg[pl.ds(nh + h, PAGE_SIZE, stride=2 * nh), :].astype(MXU_DTYPE)

    s = jnp.concatenate([jnp.concatenate([_dot_nt(qh[h], keys(pg, h)) for h in range(nh)], axis=0) for pg in pages],
                        axis=1)

    def weighted_values(p):
        acc = None
        for pi, pg in enumerate(pages):
            pp = p[:, pi * PAGE_SIZE:(pi + 1) * PAGE_SIZE]
            part = jnp.concatenate([_dot(pp[8 * h:8 * h + 8], values(pg, h)) for h in range(nh)], axis=0)
            acc = part if acc is None else acc + part
        return acc

    _flash_next(s, weighted_values, m_ref, l_ref, a_ref)

    @pl.when(j == pl.num_programs(1) - 1)
    def _():
        new = new_ref[...]
        s = jnp.concatenate([_dot_nt(qh[h], new[:, 128 * h:128 * h + 128]) for h in range(nh)], axis=0)
        s = jnp.where(_new_token_mask(8 * nh), s, NEG)
        _flash_next(s, lambda p: jnp.concatenate(
            [_dot(p[8 * h:8 * h + 8], new[:, 512 + 128 * h:640 + 128 * h]) for h in range(nh)], axis=0),
            m_ref, l_ref, a_ref)
        a = a_ref[...] / l_ref[...]
        lam = _diff_lambda(lam_ref, lam_init)
        outs = []
        for h in range(nh):
            grp = a[8 * h:8 * h + 8]
            o = grp - lam * pltpu.roll(grp, 4, 0)
            outs.append(_rms(o) * g_ref[0] * (1.0 - lam_init))
        o_ref[...] = jnp.concatenate(outs, axis=1).astype(o_ref.dtype)


def diff_sample(page_table, qrows, cache, new_kv, diff_lambda, diff_norm_g, layer, *, pages_per_step=16):
    n_seq, n_pages = page_table.shape
    lam_init = 0.8 - 0.6 * math.exp(-0.3 * layer)
    kern = functools.partial(_diff_sample_kernel, n_pg=pages_per_step, lam_init=lam_init)
    rows = 8 * DIFF_HEADS
    grid_spec = pltpu.PrefetchScalarGridSpec(
        num_scalar_prefetch=1,
        grid=(n_seq, n_pages // pages_per_step),
        in_specs=[pl.BlockSpec((None, DIFF_HEADS, 8, 128), lambda b, j, pt: (b, 0, 0, 0))]
        + _page_specs(n_pages, pages_per_step, layer, (2 * DIFF_HEADS * PAGE_SIZE, 128))
        + [pl.BlockSpec((None, 8, 1024), lambda b, j, pt: (b, 0, 0)),
           pl.BlockSpec((1, 4, HEAD_DIM), lambda b, j, pt: (layer, 0, 0)),
           pl.BlockSpec((1, 1, DIFF_VDIM), lambda b, j, pt: (layer, 0, 0))],
        out_specs=pl.BlockSpec((None, 8, 512), lambda b, j, pt: (b, 0, 0)),
        scratch_shapes=[pltpu.VMEM((rows, 1), F32), pltpu.VMEM((rows, 1), F32), pltpu.VMEM((rows, DIFF_VDIM), F32)],
    )
    return pl.pallas_call(
        kern,
        out_shape=jax.ShapeDtypeStruct((n_seq, 8, 512), MXU_DTYPE),
        grid_spec=grid_spec,
        compiler_params=_cparams("arbitrary", "arbitrary"),
        name="diff_sample",
    )(page_table.reshape(-1), qrows, *([cache] * pages_per_step), new_kv, diff_lambda, diff_norm_g)


def _moba_sample_kernel(pt_ref, q_ref, *rest, n_pg, n_blk):
    pages = rest[:n_pg]
    nk_ref, nv_ref, o_ref, m_all, l_all, acc_all, kmean = rest[n_pg:]
    j = pl.program_id(1)
    nh = MOBA_HEADS

    @pl.when(j == 0)
    def _():
        m_all[...] = jnp.full(m_all.shape, NEG, F32)
        l_all[...] = jnp.zeros(l_all.shape, F32)
        kmean[...] = jnp.zeros(kmean.shape, F32)

    width = nh * HEAD_DIM
    q = q_ref[...]
    lane = _lane(m_all.shape)
    klane = _lane(kmean.shape)
    own_head = lax.shift_right_logical(_lane((8 * nh, width)), 6) == lax.shift_right_logical(_lane((8 * nh, width), 0), 3)
    for b2 in range(n_pg // 2):
        n = j * (n_pg // 2) + b2
        pa, pb = pages[2 * b2], pages[2 * b2 + 1]
        kt = jnp.concatenate([pa[0].reshape(width, PAGE_SIZE), pb[0].reshape(width, PAGE_SIZE)], axis=1)
        vt = jnp.concatenate([pa[1].reshape(width, PAGE_SIZE), pb[1].reshape(width, PAGE_SIZE)], axis=1)
        kmean[...] = jnp.where(klane == n, jnp.sum(kt, axis=1, keepdims=True) * (1.0 / MOBA_BLOCK), kmean[...])
        s = _dot(q, kt)
        m = jnp.max(s, axis=-1, keepdims=True)
        p = jnp.exp(s - m)
        m_all[...] = jnp.where(lane == n, m, m_all[...])
        l_all[...] = jnp.where(lane == n, jnp.sum(p, axis=-1, keepdims=True), l_all[...])
        acc_all[n] = jnp.where(own_head, _dot_nt(p, vt), 0.0)

    @pl.when(j == pl.num_programs(1) - 1)
    def _():
        gate = _dot(q, kmean[...])
        sel = jnp.where(lane < n_blk, _top_k_mask(gate, MOBA_TOPK, n_blk), 0.0) > 0.5
        s_new = jnp.where(_new_token_mask(8 * nh), _dot_nt(q, nk_ref[...]), NEG)
        m_new = jnp.max(s_new, axis=-1, keepdims=True)
        p_new = jnp.exp(s_new - m_new)
        m_tot = jnp.maximum(m_new, jnp.max(jnp.where(sel, m_all[...], NEG), axis=-1, keepdims=True))
        w = jnp.where(sel, jnp.exp(m_all[...] - m_tot), 0.0)
        w_new = jnp.exp(m_new - m_tot)
        l_tot = jnp.sum(w * l_all[...], axis=-1, keepdims=True) + w_new * jnp.sum(p_new, axis=-1, keepdims=True)
        a_new = jnp.where(own_head, _dot(p_new, nv_ref[...]), 0.0)

        def body(n, acc):
            wn = jnp.sum(jnp.where(lane == n, w, 0.0), axis=-1, keepdims=True)
            return acc + wn * acc_all[n]

        o = lax.fori_loop(0, n_blk, body, w_new * a_new) / l_tot
        o_ref[...] = (o[0:8] + o[8:16] + o[16:24] + o[24:32]).astype(o_ref.dtype)


def moba_sample(page_table, qrows, cache, new_k, new_v, layer, *, pages_per_step=16):
    n_seq, n_pages = page_table.shape
    n_blk = n_pages * PAGE_SIZE // MOBA_BLOCK
    nh = MOBA_HEADS
    width = nh * HEAD_DIM
    kern = functools.partial(_moba_sample_kernel, n_pg=pages_per_step, n_blk=n_blk)
    new_tok = pl.BlockSpec((None, 8, width), lambda b, j, pt: (b, 0, 0))
    grid_spec = pltpu.PrefetchScalarGridSpec(
        num_scalar_prefetch=1,
        grid=(n_seq, n_pages // pages_per_step),
        in_specs=[pl.BlockSpec((None, 8 * nh, width), lambda b, j, pt: (b, 0, 0))]
        + _page_specs(n_pages, pages_per_step, layer, (2, nh, HEAD_DIM, PAGE_SIZE))
        + [new_tok, new_tok],
        out_specs=new_tok,
        scratch_shapes=[pltpu.VMEM((8 * nh, LANES), F32), pltpu.VMEM((8 * nh, LANES), F32),
                        pltpu.VMEM((n_blk, 8 * nh, width), F32), pltpu.VMEM((width, LANES), F32)],
    )
    return pl.pallas_call(
        kern,
        out_shape=jax.ShapeDtypeStruct((n_seq, 8, width), MXU_DTYPE),
        grid_spec=grid_spec,
        compiler_params=_cparams("arbitrary", "arbitrary"),
        name="moba_sample",
    )(page_table.reshape(-1), qrows, *([cache] * pages_per_step), new_k, new_v)


def _nsa_sample_kernel(pt_ref, qa_ref, *rest, n_pg, past, n_cmp):
    pages = rest[:n_pg]
    (win_ref, new_ref, ng_ref, pos_ref, w1_ref, w2_ref, wsel_ref, o_ref, nwin_ref,
     xs, m_all, l_all, acc_all) = rest[n_pg:]
    j = pl.program_id(1)
    n_selp = m_all.shape[1]

    @pl.when(j == 0)
    def _():
        xs[past:past + CMP_LEN, :] = jnp.zeros((CMP_LEN, 128), F32)
        m_all[...] = jnp.full(m_all.shape, NEG, F32)
        l_all[...] = jnp.zeros(l_all.shape, F32)

    qa = qa_ref[...]
    q64 = qa[:, 0:HEAD_DIM]
    lane_b = _lane(m_all.shape)
    low = _lane((16, PAGE_SIZE)) < SEL_BLOCK
    for pi, pg in enumerate(pages):
        row0 = pl.multiple_of((j * n_pg + pi) * PAGE_SIZE, PAGE_SIZE)
        xs[pl.ds(row0, PAGE_SIZE), :] = jnp.concatenate([pg[0], pg[1]], axis=0).T
        s = _dot(q64, pg[2])
        m_lo = jnp.max(jnp.where(low, s, NEG), axis=-1, keepdims=True)
        m_hi = jnp.max(jnp.where(low, NEG, s), axis=-1, keepdims=True)
        p = jnp.exp(s - jnp.where(low, m_lo, m_hi))
        p_lo, p_hi = jnp.where(low, p, 0.0), jnp.where(low, 0.0, p)
        n0 = 2 * (j * n_pg + pi)
        m_all[...] = jnp.where(lane_b == n0, m_lo, jnp.where(lane_b == n0 + 1, m_hi, m_all[...]))
        l_all[...] = jnp.where(lane_b == n0, jnp.sum(p_lo, axis=-1, keepdims=True),
                               jnp.where(lane_b == n0 + 1, jnp.sum(p_hi, axis=-1, keepdims=True), l_all[...]))
        pv = _dot_nt(jnp.concatenate([p_lo, p_hi], axis=0), pg[3])
        acc_all[n0] = pv[0:16]
        acc_all[n0 + 1] = pv[16:32]

    @pl.when(j == pl.num_programs(1) - 1)
    def _():
        new = new_ref[...]
        t16 = lax.broadcasted_iota(jnp.int32, (16, 1), 0) & (N_NEW - 1)
        new_ok = _new_token_mask(16)
        value = lambda o: pltpu.roll(o, HEAD_DIM, 1)[:, 0:HEAD_DIM]

        kvc = _compress_rows(xs, n_cmp + 1, pos_ref, w1_ref, w2_ref).astype(MXU_DTYPE)
        s = _dot_nt(qa, kvc)
        n = _lane(s.shape)
        vis = jnp.where(n * CMP_STRIDE + (CMP_LEN - 1) <= past + t16, n, n_cmp) < n_cmp
        s = jnp.where(vis, s, NEG)
        e = jnp.where(vis, jnp.exp(s - jnp.max(s, axis=-1, keepdims=True)), 0.0)
        den = jnp.sum(e, axis=-1, keepdims=True)
        p = (e / jnp.where(den > 0, den, 1.0)).astype(MXU_DTYPE)
        o_cmp = value(_dot(p, kvc))
        pw = _dot(p, wsel_ref[...])
        imp = pw[0:8] + pw[8:16]
        imp = imp + pltpu.roll(imp, 4, 0)

        own = lax.shift_right_logical(past + (lax.broadcasted_iota(jnp.int32, (8, 1), 0) & (N_NEW - 1)),
                                      int(math.log2(SEL_BLOCK)))
        jb = _lane(imp.shape)
        forced = jnp.where(jb == 0, 1, jnp.where(jb == own, 1, jnp.where(jb == own - 1, 1, 0))) > 0
        impm = jnp.where(jb <= own, jnp.where(forced, jnp.inf, imp), -jnp.inf)
        selm = jnp.where(jb <= own, _top_k_mask(impm, SEL_TOPK, imp.shape[-1]), 0.0)
        sel = jnp.concatenate([selm, selm], axis=0)[:, 0:n_selp] > 0.5

        s_new = jnp.where(new_ok, _dot_nt(qa, new[:, 128:256]), NEG)
        m_new = jnp.max(s_new, axis=-1, keepdims=True)
        p_new = jnp.exp(s_new - m_new)
        m_tot = jnp.maximum(m_new, jnp.max(jnp.where(sel, m_all[...], NEG), axis=-1, keepdims=True))
        w = jnp.where(sel, jnp.exp(m_all[...] - m_tot), 0.0)
        w_new = jnp.exp(m_new - m_tot)
        l_tot = jnp.sum(w * l_all[...], axis=-1, keepdims=True) + w_new * jnp.sum(p_new, axis=-1, keepdims=True)

        def body(nb, acc):
            wn = jnp.sum(jnp.where(lane_b == nb, w, 0.0), axis=-1, keepdims=True)
            return acc + wn * acc_all[nb]

        o_sel = lax.fori_loop(0, n_selp, body, w_new * value(_dot(p_new, new[:, 128:256]))) / l_tot

        win = win_ref[...]
        keep = win.shape[0]
        s_w = _dot_nt(qa, win)
        s_w = jnp.where(_lane(s_w.shape) > t16 + (keep - WINDOW), s_w, NEG)
        s_wn = jnp.where(new_ok, _dot_nt(qa, new[:, 256:384]), NEG)
        m_w = jnp.maximum(jnp.max(s_w, axis=-1, keepdims=True), jnp.max(s_wn, axis=-1, keepdims=True))
        p_w, p_wn = jnp.exp(s_w - m_w), jnp.exp(s_wn - m_w)
        l_w = jnp.sum(p_w, axis=-1, keepdims=True) + jnp.sum(p_wn, axis=-1, keepdims=True)
        o_win = value(_dot(p_w, win) + _dot(p_wn, new[:, 256:384])) / l_w

        g = ng_ref[...]
        top_rows = lax.broadcasted_iota(jnp.int32, (8, 1), 0) < 4

        def gate(c):
            col = lambda h: g[:, 3 * h + c:3 * h + c + 1]
            return jnp.concatenate([jnp.where(top_rows, col(0), col(1)), jnp.where(top_rows, col(2), col(3))], axis=0)

        o_ref[...] = (gate(0) * o_cmp + gate(1) * o_sel + gate(2) * o_win).astype(o_ref.dtype)

        nwin_ref[0:keep - N_NEW, :] = win_ref[N_NEW:keep, :]
        nwin_ref[keep - N_NEW:keep, :] = new[0:N_NEW, 256:384]


def nsa_sample(page_table, qa, cache, win_state, new_rows, ng8, posrows, w1p, w2p, layer, *, pages_per_step=16):
    n_seq, n_pages = page_table.shape
    past = n_pages * PAGE_SIZE
    keep = win_state.shape[2]
    total = past + N_NEW
    n_cmp = (total - CMP_LEN) // CMP_STRIDE + 1
    n_selp = past // SEL_BLOCK
    n_sel_pad = -(-(n_selp + 1) // LANES) * LANES
    wsel = jnp.asarray(_cmp_to_sel_weights(n_cmp, n_cmp + 1, n_sel_pad), MXU_DTYPE)
    kern = functools.partial(_nsa_sample_kernel, n_pg=pages_per_step, past=past, n_cmp=n_cmp)
    const = lambda shape: pl.BlockSpec(shape, lambda b, j, pt: (0,) * len(shape))
    rows = N_NEW * NSA_HEADS
    grid_spec = pltpu.PrefetchScalarGridSpec(
        num_scalar_prefetch=1,
        grid=(n_seq, n_pages // pages_per_step),
        in_specs=[pl.BlockSpec((None, rows, 128), lambda b, j, pt: (b, 0, 0))]
        + _page_specs(n_pages, pages_per_step, layer, (4, HEAD_DIM, PAGE_SIZE))
        + [pl.BlockSpec((None, None, keep, 128), lambda b, j, pt: (layer, b, 0, 0)),
           pl.BlockSpec((None, 8, 384), lambda b, j, pt: (b, 0, 0)),
           pl.BlockSpec((None, 8, 128), lambda b, j, pt: (b, 0, 0)),
           pl.BlockSpec((1, CMP_LEN, 1, 128), lambda b, j, pt: (layer, 0, 0, 0)),
           pl.BlockSpec((1, CMP_LEN // 2, 256, 512), lambda b, j, pt: (layer, 0, 0, 0)),
           pl.BlockSpec((1, 512, 128), lambda b, j, pt: (layer, 0, 0)),
           const((n_cmp + 1, n_sel_pad))],
        out_specs=(pl.BlockSpec((None, rows, HEAD_DIM), lambda b, j, pt: (b, 0, 0)),
                   pl.BlockSpec((None, keep, 128), lambda b, j, pt: (b, 0, 0))),
        scratch_shapes=[pltpu.VMEM((past + CMP_LEN, 128), F32), pltpu.VMEM((rows, n_selp), F32),
                        pltpu.VMEM((rows, n_selp), F32), pltpu.VMEM((n_selp, rows, HEAD_DIM), F32)],
    )
    return pl.pallas_call(
        kern,
        out_shape=(jax.ShapeDtypeStruct((n_seq, rows, HEAD_DIM), MXU_DTYPE),
                   jax.ShapeDtypeStruct((n_seq, keep, 128), F32)),
        grid_spec=grid_spec,
        compiler_params=_cparams("arbitrary", "arbitrary"),
        name="nsa_sample",
    )(page_table.reshape(-1), qa, *([cache] * pages_per_step), win_state, new_rows, ng8, posrows, w1p, w2p, wsel)


def sample_operands(dq, dkv, mq, mkv, nq, n4, nwin, ng, *, n_seq):
    per_seq = lambda a: a.astype(F32).reshape(n_seq, N_NEW, a.shape[-1])
    pad8 = lambda a: jnp.pad(a, [(0, 0)] * (a.ndim - 2) + [(0, 8 - N_NEW), (0, 0)])
    heads_first = lambda a, nh: a.reshape(n_seq, N_NEW, nh, a.shape[-1] // nh).transpose(0, 2, 1, 3)
    qd = heads_first(per_seq(dq), DIFF_HEADS)
    map_ok = jnp.asarray((np.arange(128) // HEAD_DIM)[None, :] == np.arange(2)[:, None], F32)
    diff_q = (qd[:, :, None, :, :] * map_ok[None, None, :, None, :]).reshape(n_seq, DIFF_HEADS, 8, 128)
    mkv_s = per_seq(mkv)
    head_ok = jnp.asarray((np.arange(256) // HEAD_DIM)[None, :] == np.arange(MOBA_HEADS)[:, None], F32)
    qn = heads_first(per_seq(nq), NSA_HEADS).reshape(n_seq, N_NEW * NSA_HEADS, HEAD_DIM)
    g = per_seq(ng)
    return {
        "diff_q": diff_q.astype(MXU_DTYPE), "diff_new": pad8(per_seq(dkv)),
        "moba_q": pad8(per_seq(mq)[:, None, :, :] * head_ok[None, :, None, :]).reshape(n_seq, 8 * MOBA_HEADS, 256)
        .astype(MXU_DTYPE),
        "moba_new_k": pad8(mkv_s[..., 0:256]), "moba_new_v": pad8(mkv_s[..., 256:512]),
        "nsa_qa": jnp.concatenate([qn, jnp.zeros_like(qn)], axis=-1).astype(MXU_DTYPE),
        "nsa_new": pad8(jnp.concatenate([per_seq(n4), per_seq(nwin)], axis=-1)),
        "nsa_g": jnp.concatenate([g, g], axis=1),
    }


def _tokens_first(o, n_seq, nh, rows_per_head):
    o = o.reshape(n_seq, nh, rows_per_head, o.shape[-1])[:, :, :N_NEW]
    return o.transpose(0, 2, 1, 3).reshape(n_seq * N_NEW, nh * o.shape[-1])


def kernel(x_prompt, x_sample, cache_diff_kv, cache_moba_kv, cache_nsa_kv, state_nsa_win, page_table, c_prompt, c_sample, w_ada, b_ada, g_mix, g_ffn, w_in, diff_lambda, diff_norm_g, cmp_pos, cmp_w1, cmp_w2, w_br_a, w_br_b, w_br_c, w_out, router_w, router_b, w_e_gate, w_e_up, w_e_down, g_final):
    batch, seq, d = x_prompt.shape
    n_seq, n_new, _ = x_sample.shape
    assert n_new == N_NEW and d == D_MODEL and seq % MOBA_BLOCK == 0
    depth = w_ada.shape[0]
    n_phys = cache_diff_kv.shape[1]
    past = page_table.shape[1] * PAGE_SIZE
    keep = state_nsa_win.shape[2]
    rows_s = n_seq * n_new

    n_cond = batch + n_seq
    c_all = jnp.concatenate([c_prompt, c_sample, jnp.zeros((-n_cond % 8, d), F32)], axis=0)
    mod = ada_ln(c_all, w_ada, b_ada)
    w1 = jnp.pad(w_in[:, :, :PROJ_COLS], ((0, 0), (0, 0), (0, PROJ_PAD - PROJ_COLS))).astype(MXU_DTYPE)
    wbg = w_in[:, :, PROJ_COLS:].astype(MXU_DTYPE)
    wa, wb, wo = w_br_a.astype(MXU_DTYPE), w_br_b.astype(MXU_DTYPE), w_out.astype(MXU_DTYPE)
    wc4 = w_br_c.reshape(depth, NSA_HEADS, HEAD_DIM, d)
    wc_wide = jnp.concatenate([jnp.zeros_like(wc4), wc4], axis=2).reshape(depth, NSA_HEADS * 128, d).astype(MXU_DTYPE)
    wc = w_br_c.astype(MXU_DTYPE)
    rw = jnp.pad(router_w, ((0, 0), (0, LANES - N_EXPERTS)))
    rb = jnp.pad(router_b, (0, LANES - N_EXPERTS)).reshape(1, LANES)
    wg, wu, wd = w_e_gate.astype(MXU_DTYPE), w_e_up.astype(MXU_DTYPE), w_e_down.astype(MXU_DTYPE)
    posrows, w1p, w2p = compress_weights(cmp_pos, cmp_w1, cmp_w2)
    g_mix3, g_ffn3 = g_mix.reshape(depth, 1, d), g_ffn.reshape(depth, 1, d)
    gf = g_final.reshape(1, d)
    norm_g_row = diff_norm_g.reshape(depth, 1, DIFF_VDIM)
    norm_g_col = diff_norm_g.reshape(depth, DIFF_VDIM, 1)
    tables_p = rope_tables(jnp.arange(seq, dtype=jnp.int32))
    tables_s = rope_tables(jnp.tile(past + jnp.arange(n_new, dtype=jnp.int32), n_seq))
    cache_d = cache_diff_kv.reshape(depth, n_phys, PAGE_SIZE * 2 * DIFF_HEADS, DIFF_VDIM)
    cache_m = jnp.transpose(cache_moba_kv, (0, 1, 3, 4, 5, 2))
    cache_n = jnp.transpose(cache_nsa_kv.reshape(depth, n_phys, PAGE_SIZE, 4, HEAD_DIM), (0, 1, 3, 4, 2))
    win_state = state_nsa_win.reshape(depth, n_seq, keep, 2 * HEAD_DIM)

    xp = x_prompt.reshape(batch * seq, d)
    xs = x_sample.reshape(rows_s, d)
    prompt = dict(rows_per_batch=seq, per_token=False)
    sample = dict(tm=rows_s, rows_per_batch=rows_s, per_token=True)
    leaves = [[] for _ in range(8)]
    for l in range(depth):
        last = l == depth - 1
        modp = mod[l, :batch].reshape(batch, 1, 6 * d)
        dq, dkv, mq, mkv, nq, n4, nwin, ng = project(xp, l, g_mix3, modp, modp, w1, tables_p, tm=256, **prompt)
        oa = diff_prompt(dq, dkv, diff_lambda, norm_g_col, l, batch=batch, seq=seq)
        ob = moba_prompt(mq, mkv, batch=batch, seq=seq)
        kvc = compress_prompt(n4, posrows, w1p, w2p, l, batch=batch, seq=seq)
        oc = nsa_prompt(nq, kvc, n4, nwin, ng, batch=batch, seq=seq)
        xp = merge(xp, oa, ob, oc, l, g_mix3, modp, wbg, wa, wb, wc_wide, wo, tm=512, **prompt)
        xp = moe(xp, l, g_ffn3, modp, rw, rb, wg, wu, wd, gf, tm=1024, final=last, **prompt)
        leaves[0].append(dkv.reshape(batch, seq, 2, DIFF_HEADS, DIFF_VDIM))
        leaves[2].append(mkv.reshape(batch, seq, 2, MOBA_HEADS, HEAD_DIM))
        leaves[4].append(n4.reshape(batch, seq, 4, 1, HEAD_DIM))
        leaves[6].append(nwin.reshape(batch, seq, 2, 1, HEAD_DIM)[:, seq - min(WINDOW, seq):])
        mods = jnp.repeat(mod[l, batch:n_cond], n_new, axis=0)
        dq, dkv, mq, mkv, nq, n4, nwin, ng = project(xs, l, g_mix3, mods, mods, w1, tables_s, **sample)
        ops = sample_operands(dq, dkv, mq, mkv, nq, n4, nwin, ng, n_seq=n_seq)
        oa = diff_sample(page_table, ops["diff_q"], cache_d, ops["diff_new"], diff_lambda, norm_g_row, l)
        ob = moba_sample(page_table, ops["moba_q"], cache_m, ops["moba_new_k"], ops["moba_new_v"], l)
        oc, new_win = nsa_sample(page_table, ops["nsa_qa"], cache_n, win_state, ops["nsa_new"], ops["nsa_g"],
                                 posrows, w1p, w2p, l)
        oa = oa[:, :n_new].reshape(rows_s, DIFF_HEADS * DIFF_VDIM)
        ob = ob[:, :n_new].reshape(rows_s, MOBA_HEADS * HEAD_DIM)
        oc = _tokens_first(oc, n_seq, NSA_HEADS, N_NEW)
        xs = merge(xs, oa, ob, oc, l, g_mix3, mods, wbg, wa, wb, wc, wo, **sample)
        xs = moe(xs, l, g_ffn3, mods, rw, rb, wg, wu, wd, gf, final=last, **sample)
        leaves[1].append(dkv.reshape(n_seq, n_new, 2, DIFF_HEADS, DIFF_VDIM))
        leaves[3].append(mkv.reshape(n_seq, n_new, 2, MOBA_HEADS, HEAD_DIM))
        leaves[5].append(n4.reshape(n_seq, n_new, 4, 1, HEAD_DIM))
        leaves[7].append(new_win.reshape(n_seq, keep, 2, 1, HEAD_DIM))
    return (xp.reshape(batch, seq, d), xs.reshape(n_seq, n_new, d)) + tuple(jnp.stack(v) for v in leaves)
```

```python
import functools
import math

import jax
import jax.numpy as jnp
import numpy as np
from jax import lax
from jax.experimental import pallas as pl
from jax.experimental.pallas import tpu as pltpu

D_MODEL = 1024
HEAD_DIM = 64
HALF = HEAD_DIM // 2
ROPE_THETA = 10000.0
RMS_EPS = 1e-6
PAGE_SIZE = 128
DIFF_HEADS = 4
DIFF_VDIM = 2 * HEAD_DIM
MOBA_HEADS = 4
MOBA_BLOCK = 256
MOBA_TOPK = 3
NSA_HEADS = 4
CMP_LEN = 32
CMP_STRIDE = 16
CMP_HIDDEN = 256
SEL_BLOCK = 64
SEL_TOPK = 16
WINDOW = 512
N_BRANCH = 3
N_NEW = 4
N_EXPERTS = 16
N_GROUPS = 4
EXPERTS_PER_GROUP = N_EXPERTS // N_GROUPS
EXPERT_FF = 512
PROJ_COLS = 2956
PROJ_PAD = 3072
Q_SCALE = HEAD_DIM ** -0.5 * math.log2(math.e)

LANES = 128
VMEM_LIMIT_BYTES = 56 * 1024 * 1024

MXU_DTYPE = jnp.bfloat16
F32 = jnp.float32
NEG = -1e30


def _cparams(*sem):
    return pltpu.CompilerParams(dimension_semantics=sem, vmem_limit_bytes=VMEM_LIMIT_BYTES)


def _sigmoid(x):
    return 1.0 / (1.0 + jnp.exp(-x))


def _silu(x):
    return x * _sigmoid(x)


def _gelu_tanh(x):
    c = math.sqrt(2.0 / math.pi)
    return 0.5 * x * (1.0 + jnp.tanh(c * (x + 0.044715 * (x * x * x))))


def _dot(a, b):
    return jnp.dot(a.astype(MXU_DTYPE), b.astype(MXU_DTYPE), preferred_element_type=F32)


def _dot_nt(a, b):
    return lax.dot_general(a.astype(MXU_DTYPE), b.astype(MXU_DTYPE), (((1,), (1,)), ((), ())),
                           preferred_element_type=F32)


def _rms(x):
    return x * lax.rsqrt(jnp.mean(x * x, axis=-1, keepdims=True) + RMS_EPS)


def _mod_norm(x, g, shift, scale):
    return (_rms(x) * g) * (1.0 + scale) + shift


def _lane(shape, dim=None):
    return lax.broadcasted_iota(jnp.int32, shape, len(shape) - 1 if dim is None else dim)


def _rope(y, cos_t, sin_t):
    outs = []
    for c in range(y.shape[1] // LANES):
        blk = y[:, c * LANES:(c + 1) * LANES]
        first = (_lane(blk.shape) % HEAD_DIM) < HALF
        partner = jnp.where(first, pltpu.roll(blk, LANES - HALF, 1), pltpu.roll(blk, HALF, 1))
        outs.append(blk * cos_t + partner * sin_t)
    return outs[0] if len(outs) == 1 else jnp.concatenate(outs, axis=1)


def _flash_next(s, v_fn, m_ref, l_ref, acc_ref):
    m_old = m_ref[...]
    m = jnp.maximum(m_old, jnp.max(s, axis=-1, keepdims=True))
    alpha = jnp.exp2(m_old - m)
    p = jnp.exp2(s - m)
    m_ref[...] = m
    l_ref[...] = alpha * l_ref[...] + jnp.sum(p, axis=-1, keepdims=True)
    acc_ref[...] = alpha * acc_ref[...] + v_fn(p)


def _flash_t_next(s, vt, m_ref, l_ref, acc_ref):
    m_old = m_ref[...]
    m = jnp.maximum(m_old, jnp.max(s, axis=0, keepdims=True))
    alpha = jnp.exp2(m_old - m)
    p = jnp.exp2(s - m)
    m_ref[...] = m
    l_ref[...] = alpha * l_ref[...] + jnp.sum(p, axis=0, keepdims=True)
    acc_ref[...] = alpha * acc_ref[...] + _dot(vt, p)


def _flash_t_loop(score_fn, vt_ref, stats, sbuf, first_blk, lo, hi):
    for m_ref, l_ref, acc_ref in stats:
        m_ref[...] = jnp.full(m_ref.shape, NEG, F32)
        l_ref[...] = jnp.zeros(l_ref.shape, F32)
        acc_ref[...] = jnp.zeros(acc_ref.shape, F32)

    if sbuf is None:
        def direct(blk, first):
            vt = vt_ref[blk]
            for s, (m_ref, l_ref, acc_ref) in zip(score_fn(blk, first), stats):
                _flash_t_next(s, vt, m_ref, l_ref, acc_ref)

        direct(first_blk, True)

        def body(t, carry):
            direct(t, False)
            return carry

        lax.fori_loop(lo, hi, body, 0)
        return

    for mi, s in enumerate(score_fn(first_blk, True)):
        sbuf[0, mi] = s

    def consume(slot, blk):
        vt = vt_ref[blk]
        for mi, (m_ref, l_ref, acc_ref) in enumerate(stats):
            _flash_t_next(sbuf[slot, mi], vt, m_ref, l_ref, acc_ref)

    def body(t, carry):
        k = t - lo
        nxt = score_fn(t, False)
        consume(k & 1, jnp.where(k == 0, first_blk, t - 1))
        for mi, s in enumerate(nxt):
            sbuf[1 - (k & 1), mi] = s
        return carry

    lax.fori_loop(lo, hi, body, 0)
    n = hi - lo
    consume(n & 1, jnp.where(n == 0, first_blk, hi - 1))


def _top_k_rows(vals, k):
    row = _lane(vals.shape, 0)
    big = jnp.int32(vals.shape[0])

    def body(_, carry):
        sel, cur = carry
        m = jnp.max(cur, axis=0, keepdims=True)
        idx = jnp.min(jnp.where(cur == m, row, big), axis=0, keepdims=True)
        hit = row == idx
        return jnp.where(hit, 1.0, sel), jnp.where(hit, -jnp.inf, cur)

    sel, _ = lax.fori_loop(0, k, body, (jnp.zeros(vals.shape, F32), vals))
    return sel


def _top_k_mask(vals, k, n_valid_lanes):
    lane = _lane(vals.shape)
    big = jnp.int32(vals.shape[-1])

    def body(_, carry):
        sel, cur = carry
        m = jnp.max(cur, axis=-1, keepdims=True)
        idx = jnp.min(jnp.where(cur == m, lane, big), axis=-1, keepdims=True)
        hit = lane == idx
        return jnp.where(hit, 1.0, sel), jnp.where(hit, -jnp.inf, cur)

    cur0 = jnp.where(lane < n_valid_lanes, vals, -jnp.inf)
    sel, _ = lax.fori_loop(0, k, body, (jnp.zeros(vals.shape, F32), cur0))
    return sel


def _ada_kernel(c_ref, w_ref, b_ref, o_ref):
    o_ref[0] = _dot(_silu(c_ref[...]), w_ref[0]) + b_ref[0]


def ada_ln(c_all, w_ada, b_ada):
    depth, _, n = w_ada.shape
    rc = c_all.shape[0]
    tn = 1536
    return pl.pallas_call(
        _ada_kernel,
        out_shape=jax.ShapeDtypeStruct((depth, rc, n), F32),
        grid=(depth, n // tn),
        in_specs=[pl.BlockSpec((rc, D_MODEL), lambda l, j: (0, 0)),
                  pl.BlockSpec((1, D_MODEL, tn), lambda l, j: (l, 0, j)),
                  pl.BlockSpec((1, 1, tn), lambda l, j: (l, 0, j))],
        out_specs=pl.BlockSpec((1, rc, tn), lambda l, j: (l, 0, j)),
        compiler_params=_cparams("arbitrary", "arbitrary"),
        name="ada_ln",
    )(c_all, w_ada, b_ada.reshape(depth, 1, n))


def _mod_spec(per_token, tm, rows_per_batch_tile, chunk):
    if per_token:
        return pl.BlockSpec((tm, D_MODEL), lambda i: (i, chunk))
    return pl.BlockSpec((None, 1, D_MODEL), lambda i: (i // rows_per_batch_tile, 0, chunk))


def _proj_kernel(x_ref, g_ref, sh_ref, sc_ref, w_ref, cf_ref, sf_ref, ck_ref, sk_ref,
                 dq_ref, dkv_ref, mq_ref, mkv_ref, nq_ref, n4_ref, nwin_ref, ng_ref):
    h = _mod_norm(x_ref[...], g_ref[0], sh_ref[...], sc_ref[...])
    y = _dot(h, w_ref[0])
    cf, sf, ck, sk = cf_ref[...], sf_ref[...], ck_ref[...], sk_ref[...]
    dq_ref[...] = (_rope(y[:, 0:512], cf, sf) * Q_SCALE).astype(dq_ref.dtype)
    dkv_ref[:, 0:512] = _rope(y[:, 512:1024], cf, sf)
    dkv_ref[:, 512:1024] = y[:, 1024:1536]
    mq_ref[...] = (_rope(y[:, 1536:1792], cf, sf) * Q_SCALE).astype(mq_ref.dtype)
    mkv_ref[:, 0:256] = _rope(y[:, 1792:2048], cf, sf)
    mkv_ref[:, 256:512] = y[:, 2048:2304]
    nq_ref[...] = _rope(y[:, 2304:2560], cf, sf) * Q_SCALE
    n4_ref[...] = _rope(y[:, 2560:2816], ck, sk)
    nwin_ref[...] = _rope(y[:, 2816:2944], ck, sk)
    ng_ref[...] = _sigmoid(y[:, 2944:3072])


def project(x, layer, g_mix, shift, scale, w1, tables, *, tm, rows_per_batch, per_token):
    r = x.shape[0]
    nt = rows_per_batch // tm
    row = lambda w: pl.BlockSpec((tm, w), lambda i: (i, 0))
    tab = pl.BlockSpec((tm, LANES), lambda i: (i % nt, 0))
    out_shapes = (
        jax.ShapeDtypeStruct((r, 512), MXU_DTYPE),
        jax.ShapeDtypeStruct((r, 1024), F32),
        jax.ShapeDtypeStruct((r, 256), MXU_DTYPE),
        jax.ShapeDtypeStruct((r, 512), F32),
        jax.ShapeDtypeStruct((r, 256), F32),
        jax.ShapeDtypeStruct((r, 256), F32),
        jax.ShapeDtypeStruct((r, 128), F32),
        jax.ShapeDtypeStruct((r, 128), F32),
    )
    return pl.pallas_call(
        _proj_kernel,
        out_shape=out_shapes,
        grid=(r // tm,),
        in_specs=[row(D_MODEL),
                  pl.BlockSpec((1, 1, D_MODEL), lambda i: (layer, 0, 0)),
                  _mod_spec(per_token, tm, nt, 0), _mod_spec(per_token, tm, nt, 1),
                  pl.BlockSpec((1, D_MODEL, PROJ_PAD), lambda i: (layer, 0, 0)),
                  tab, tab, tab, tab],
        out_specs=tuple(row(s.shape[1]) for s in out_shapes),
        compiler_params=_cparams("arbitrary"),
        name="project",
    )(x, g_mix, shift, scale, w1, *tables)


def rope_tables(pos):
    freqs = ROPE_THETA ** (-jnp.arange(HALF, dtype=F32) / HALF)
    ang = pos.astype(F32)[:, None] * freqs[None, :]
    cos, sin = jnp.cos(ang), jnp.sin(ang)
    c64 = jnp.concatenate([cos, cos], axis=1)
    s64 = jnp.concatenate([-sin, sin], axis=1)
    one, zero = jnp.ones_like(c64), jnp.zeros_like(s64)
    return (jnp.concatenate([c64, c64], axis=1), jnp.concatenate([s64, s64], axis=1),
            jnp.concatenate([c64, one], axis=1), jnp.concatenate([s64, zero], axis=1))


def _diff_lambda(lam_ref, lam_init):
    lp = lam_ref[0]
    a = jnp.sum(lp[0:1] * lp[1:2], axis=-1, keepdims=True)
    b = jnp.sum(lp[2:3] * lp[3:4], axis=-1, keepdims=True)
    return jnp.exp(a) - jnp.exp(b) + lam_init


def _fill_kv(k_ref, v_ref, kb, vt):
    kb[...] = k_ref[...].astype(kb.dtype)
    tk = vt.shape[2]
    for c in range(vt.shape[0]):
        vt[c] = v_ref[c * tk:(c + 1) * tk, :].T.astype(vt.dtype)


def _split_heads_t(q_ref):
    qt = q_ref[...].astype(F32).T
    top = _lane(qt.shape, 0) < HEAD_DIM
    return jnp.where(top, qt, 0.0).astype(MXU_DTYPE), jnp.where(top, 0.0, qt).astype(MXU_DTYPE)


def _diff_prompt_kernel(q_ref, k_ref, v_ref, lam_ref, g_ref, o_ref,
                        kb, vt, m1, l1, a1, m2, l2, a2, *, tq, tk, lam_init):
    i = pl.program_id(2)

    @pl.when(i == 0)
    def _():
        _fill_kv(k_ref, v_ref, kb, vt)

    qs = _split_heads_t(q_ref)
    q0 = i * tq
    jd = q0 // tk

    def scores(j, diag):
        j0 = pl.multiple_of(j * tk, tk)
        kj = kb[pl.ds(j0, tk), :]
        out = [_dot(kj, qm) for qm in qs]
        if diag:
            causal = j0 + _lane((tk, tq), 0) <= q0 + _lane((tk, tq), 1)
            out = [jnp.where(causal, s, NEG) for s in out]
        return out

    _flash_t_loop(scores, vt, ((m1, l1, a1), (m2, l2, a2)), None, jd, 0, jd)

    lam = _diff_lambda(lam_ref, lam_init)
    o = a1[...] / l1[...] - lam * (a2[...] / l2[...])
    o = o * lax.rsqrt(jnp.mean(o * o, axis=0, keepdims=True) + RMS_EPS) * g_ref[0] * (1.0 - lam_init)
    o_ref[...] = o.T.astype(o_ref.dtype)


def diff_prompt(dq, dkv, diff_lambda, diff_norm_g, layer, *, batch, seq, tq=512, tk=512):
    nq = seq // tq
    lam_init = 0.8 - 0.6 * math.exp(-0.3 * layer)
    kern = functools.partial(_diff_prompt_kernel, tq=tq, tk=tk, lam_init=lam_init)
    stat = pltpu.VMEM((1, tq), F32)
    acc = pltpu.VMEM((DIFF_VDIM, tq), F32)
    return pl.pallas_call(
        kern,
        out_shape=jax.ShapeDtypeStruct((batch * seq, DIFF_HEADS * DIFF_VDIM), MXU_DTYPE),
        grid=(batch, DIFF_HEADS, nq),
        in_specs=[pl.BlockSpec((tq, 128), lambda b, h, i: (b * nq + i, h)),
                  pl.BlockSpec((seq, 128), lambda b, h, i: (b, h)),
                  pl.BlockSpec((seq, 128), lambda b, h, i: (b, DIFF_HEADS + h)),
                  pl.BlockSpec((1, 4, HEAD_DIM), lambda b, h, i: (layer, 0, 0)),
                  pl.BlockSpec((1, DIFF_VDIM, 1), lambda b, h, i: (layer, 0, 0))],
        out_specs=pl.BlockSpec((tq, 128), lambda b, h, i: (b * nq + i, h)),
        scratch_shapes=[pltpu.VMEM((seq, 128), MXU_DTYPE), pltpu.VMEM((seq // tk, 128, tk), MXU_DTYPE),
                        stat, stat, acc, stat, stat, acc],
        compiler_params=_cparams("arbitrary", "arbitrary", "arbitrary"),
        name="diff_prompt",
    )(dq, dkv, dkv, diff_lambda, diff_norm_g)


def _moba_prompt_kernel(q_ref, k_ref, v_ref, o_ref, kb, vt, kmean, sel1, sel2, sbuf, m1, l1, a1, m2, l2, a2,
                        *, n_blk, tk):
    i = pl.program_id(2)
    blk = MOBA_BLOCK
    per = tk // blk

    @pl.when(i == 0)
    def _():
        _fill_kv(k_ref, v_ref, kb, vt)
        for n in range(n_blk):
            kmean[n:n + 1, :] = jnp.sum(k_ref[n * blk:(n + 1) * blk, :], axis=0, keepdims=True) * (1.0 / blk)

    qs = _split_heads_t(q_ref)
    sels = (sel1, sel2)
    km = kmean[...]
    brow = _lane((n_blk, blk), 0)
    for qm, sel_ref in zip(qs, sels):
        gate = jnp.where(brow < i, _dot(km, qm), -jnp.inf)
        sel_ref[...] = jnp.where(brow < i, _top_k_rows(gate, MOBA_TOPK), 0.0)

    causal = jnp.where(_lane((blk, blk), 0) <= _lane((blk, blk), 1), 1.0, 0.0)

    def scores(j, _):
        kj = kb[pl.ds(pl.multiple_of(j * tk, tk), tk), :]
        raw = [_dot(kj, qm) for qm in qs]
        out = []
        for s, sel_ref in zip(raw, sels):
            on = []
            for u in range(per):
                n = j * per + u
                past = jnp.broadcast_to(sel_ref[pl.ds(n, 1), :], (blk, blk))
                on.append(jnp.where(n == i, causal, jnp.where(n < i, past, 0.0)))
            out.append(jnp.where(jnp.concatenate(on, axis=0) > 0.5, s, NEG))
        return out

    jd = i // per
    _flash_t_loop(scores, vt, ((m1, l1, a1), (m2, l2, a2)), sbuf, jd, 0, jd)
    o = jnp.where(_lane((128, blk), 0) < HEAD_DIM, a1[...] / l1[...], a2[...] / l2[...])
    o_ref[...] = o.T.astype(o_ref.dtype)


def moba_prompt(mq, mkv, *, batch, seq, tk=512):
    blk = MOBA_BLOCK
    nq = seq // blk
    assert seq % tk == 0
    kern = functools.partial(_moba_prompt_kernel, n_blk=nq, tk=tk)
    stat = pltpu.VMEM((1, blk), F32)
    acc = pltpu.VMEM((128, blk), F32)
    sel = pltpu.VMEM((nq, blk), F32)
    pairs = MOBA_HEADS // 2
    return pl.pallas_call(
        kern,
        out_shape=jax.ShapeDtypeStruct((batch * seq, MOBA_HEADS * HEAD_DIM), MXU_DTYPE),
        grid=(batch, pairs, nq),
        in_specs=[pl.BlockSpec((blk, 128), lambda b, h, i: (b * nq + i, h)),
                  pl.BlockSpec((seq, 128), lambda b, h, i: (b, h)),
                  pl.BlockSpec((seq, 128), lambda b, h, i: (b, pairs + h))],
        out_specs=pl.BlockSpec((blk, 128), lambda b, h, i: (b * nq + i, h)),
        scratch_shapes=[pltpu.VMEM((seq, 128), MXU_DTYPE), pltpu.VMEM((seq // tk, 128, tk), MXU_DTYPE),
                        pltpu.VMEM((nq, 128), F32), sel, sel, pltpu.VMEM((2, 2, tk, blk), F32),
                        stat, stat, acc, stat, stat, acc],
        compiler_params=_cparams("arbitrary", "arbitrary", "arbitrary"),
        name="moba_prompt",
    )(mq, mkv, mkv)


def compress_weights(cmp_pos, cmp_w1, cmp_w2):
    depth = cmp_w1.shape[0]
    w1k = cmp_w1[:, 0].reshape(depth, CMP_LEN, HEAD_DIM, CMP_HIDDEN)
    w1v = cmp_w1[:, 1].reshape(depth, CMP_LEN, HEAD_DIM, CMP_HIDDEN)
    z = jnp.zeros_like(w1k)
    per_row = jnp.concatenate([jnp.concatenate([w1k, z], axis=-1), jnp.concatenate([z, w1v], axis=-1)], axis=2)
    w1p = per_row.reshape(depth, CMP_LEN // 2, 4 * HEAD_DIM, 2 * CMP_HIDDEN).astype(MXU_DTYPE)
    posrows = jnp.transpose(cmp_pos, (0, 2, 1, 3)).reshape(depth, CMP_LEN, 1, 2 * HEAD_DIM)
    z2 = jnp.zeros_like(cmp_w2[:, 0])
    w2p = jnp.concatenate([jnp.concatenate([cmp_w2[:, 0], z2], axis=-1),
                           jnp.concatenate([z2, cmp_w2[:, 1]], axis=-1)], axis=1).astype(MXU_DTYPE)
    return posrows, w1p, w2p


def _compress_rows(xs, n_tok, pos_ref, w1_ref, w2_ref):
    acc = jnp.zeros((n_tok, 2 * CMP_HIDDEN), F32)
    for p in range(CMP_LEN // 2):
        xa = xs[pl.ds(2 * p, n_tok, stride=CMP_STRIDE), :] + pos_ref[0, 2 * p]
        xb = xs[pl.ds(2 * p + 1, n_tok, stride=CMP_STRIDE), :] + pos_ref[0, 2 * p + 1]
        acc = acc + _dot(jnp.concatenate([xa, xb], axis=1), w1_ref[0, p])
    return _dot(_gelu_tanh(acc), w2_ref[0])


def _compress_prompt_kernel(x_ref, pos_ref, w1_ref, w2_ref, o_ref, xs, *, seq, n_tok):
    xs[0:seq, :] = x_ref[...]
    xs[seq:seq + CMP_LEN, :] = jnp.zeros((CMP_LEN, 128), F32)
    o_ref[...] = _compress_rows(xs, n_tok, pos_ref, w1_ref, w2_ref)


def compress_prompt(n4, posrows, w1p, w2p, layer, *, batch, seq):
    n_tok = seq // CMP_STRIDE
    kern = functools.partial(_compress_prompt_kernel, seq=seq, n_tok=n_tok)
    return pl.pallas_call(
        kern,
        out_shape=jax.ShapeDtypeStruct((batch, n_tok, 128), F32),
        grid=(batch,),
        in_specs=[pl.BlockSpec((seq, 128), lambda b: (b, 0)),
                  pl.BlockSpec((1, CMP_LEN, 1, 128), lambda b: (layer, 0, 0, 0)),
                  pl.BlockSpec((1, CMP_LEN // 2, 256, 512), lambda b: (layer, 0, 0, 0)),
                  pl.BlockSpec((1, 512, 128), lambda b: (layer, 0, 0))],
        out_specs=pl.BlockSpec((None, n_tok, 128), lambda b: (b, 0, 0)),
        scratch_shapes=[pltpu.VMEM((seq + CMP_LEN, 128), F32)],
        compiler_params=_cparams("arbitrary"),
        name="compress_prompt",
    )(n4, posrows, w1p, w2p)


def _cmp_to_sel_weights(n_cmp, n_cmp_pad, n_sel_pad):
    ratio_sel = SEL_BLOCK // CMP_STRIDE
    ratio_cmp = CMP_LEN // CMP_STRIDE
    off = np.arange(n_cmp)[:, None] - ratio_sel * np.arange(n_sel_pad)[None, :]
    shifts = (np.arange(ratio_sel)[:, None] - np.arange(ratio_cmp)[None, :]).reshape(-1)
    w = np.sum(off[..., None] == shifts, axis=-1).astype(np.float32)
    return np.pad(w, ((0, n_cmp_pad - n_cmp), (0, 0)))


def _nsa_prompt_kernel(q_ref, kvc_ref, sel_ref, win_ref, ng_ref, wselt_ref, o_ref,
                       kvc_b, kvc_t, sel_b, sel_t, win_b, win_t, selm, sbuf, m_s, l_s, a_s, m_w, l_w, a_w,
                       *, tq, tk, n_cmp):
    i = pl.program_id(1)
    nh = NSA_HEADS

    @pl.when(i == 0)
    def _():
        kvc = kvc_ref[...]
        kvc_b[...] = kvc.astype(kvc_b.dtype)
        kvc_t[...] = kvc.T.astype(kvc_t.dtype)
        _fill_kv(sel_ref, sel_ref, sel_b, sel_t)
        _fill_kv(win_ref, win_ref, win_b, win_t)

    qt = q_ref[...].T
    zero = jnp.zeros((HEAD_DIM, tq), F32)
    qp = jnp.concatenate([jnp.concatenate([qt[HEAD_DIM * h:HEAD_DIM * (h + 1)], zero], axis=0) for h in range(nh)],
                         axis=1).astype(MXU_DTYPE)
    cols = nh * tq
    q0 = i * tq
    pos = q0 + (_lane((1, cols)) & (tq - 1))
    pos_q = q0 + _lane((1, tq))

    s = _dot(kvc_b[...], qp)
    n = _lane(s.shape, 0)
    vis = jnp.where(n * CMP_STRIDE + (CMP_LEN - 1) <= pos, n, n_cmp) < n_cmp
    s = jnp.where(vis, s, NEG)
    e = jnp.where(vis, jnp.exp2(s - jnp.max(s, axis=0, keepdims=True)), 0.0)
    den = jnp.sum(e, axis=0, keepdims=True)
    p = (e / jnp.where(den > 0, den, 1.0)).astype(MXU_DTYPE)
    o_cmp = _dot(kvc_t[...], p)
    wselt = wselt_ref[...]
    imp = _dot(wselt, p[:, 0:tq])
    for h in range(1, nh):
        imp = imp + _dot(wselt, p[:, h * tq:(h + 1) * tq])

    own = lax.shift_right_logical(pos_q, int(math.log2(SEL_BLOCK)))
    jb = _lane(imp.shape, 0)
    forced = jnp.where(jb == 0, 1, jnp.where(jb == own, 1, jnp.where(jb == own - 1, 1, 0))) > 0
    impm = jnp.where(jb <= own, jnp.where(forced, jnp.inf, imp), -jnp.inf)
    selm[...] = jnp.where(jb <= own, _top_k_rows(impm, SEL_TOPK), 0.0)

    jd = q0 // tk
    per = tk // SEL_BLOCK

    def sel_scores(j, diag):
        j0 = pl.multiple_of(j * tk, tk)
        s = _dot(sel_b[pl.ds(j0, tk), :], qp)
        on = jnp.concatenate([jnp.broadcast_to(selm[pl.ds(j * per + u, 1), :], (SEL_BLOCK, tq)) for u in range(per)],
                             axis=0)
        on = jnp.concatenate([on] * nh, axis=1)
        if diag:
            on = jnp.where(j0 + _lane((tk, cols), 0) <= pos, on, 0.0)
        return [jnp.where(on > 0.5, s, NEG)]

    _flash_t_loop(sel_scores, sel_t, ((m_s, l_s, a_s),), sbuf, jd, 0, jd)

    def win_scores(j, _):
        j0 = pl.multiple_of(j * tk, tk)
        s = _dot(win_b[pl.ds(j0, tk), :], qp)
        key = j0 + _lane((tk, cols), 0)
        on = jnp.where(key <= pos, jnp.where(key > pos - WINDOW, 1, 0), 0) > 0
        return [jnp.where(on, s, NEG)]

    _flash_t_loop(win_scores, win_t, ((m_w, l_w, a_w),), sbuf, jd, jnp.maximum(q0 - (WINDOW - 1), 0) // tk, jd)

    n_pad = jnp.maximum(WINDOW - 1 - pos, 0).astype(F32)
    m_win = m_w[...]
    m_all = jnp.where(n_pad > 0, jnp.maximum(m_win, 0.0), m_win)
    shrink = jnp.exp2(m_win - m_all)
    o_win = (a_w[...] * shrink) / (l_w[...] * shrink + n_pad * jnp.exp2(-m_all))

    gt = ng_ref[...].T

    def gate(c):
        return jnp.concatenate([gt[3 * h + c:3 * h + c + 1, :] for h in range(nh)], axis=1)

    o = gate(0) * o_cmp + gate(1) * (a_s[...] / l_s[...]) + gate(2) * o_win
    o = jnp.where(_lane(o.shape, 0) >= HEAD_DIM, o, 0.0)
    o_ref[...] = jnp.concatenate([o[:, h * tq:(h + 1) * tq].T for h in range(nh)], axis=1).astype(o_ref.dtype)


def nsa_prompt(nq, kvc, n4, nwin, ng, *, batch, seq, tq=256, tk=512):
    nqb = seq // tq
    n_cmp = (seq - CMP_LEN) // CMP_STRIDE + 1
    n_cmp_pad = kvc.shape[1]
    wselt = jnp.asarray(_cmp_to_sel_weights(n_cmp, n_cmp_pad, 128).T, MXU_DTYPE)
    cols = NSA_HEADS * tq
    kern = functools.partial(_nsa_prompt_kernel, tq=tq, tk=tk, n_cmp=n_cmp)
    stat = pltpu.VMEM((1, cols), F32)
    acc = pltpu.VMEM((128, cols), F32)
    rows = pltpu.VMEM((seq, 128), MXU_DTYPE)
    trans = pltpu.VMEM((seq // tk, 128, tk), MXU_DTYPE)
    return pl.pallas_call(
        kern,
        out_shape=jax.ShapeDtypeStruct((batch * seq, NSA_HEADS * 128), MXU_DTYPE),
        grid=(batch, nqb),
        in_specs=[pl.BlockSpec((tq, 256), lambda b, i: (b * nqb + i, 0)),
                  pl.BlockSpec((None, n_cmp_pad, 128), lambda b, i: (b, 0, 0)),
                  pl.BlockSpec((seq, 128), lambda b, i: (b, 1)),
                  pl.BlockSpec((seq, 128), lambda b, i: (b, 0)),
                  pl.BlockSpec((tq, 128), lambda b, i: (b * nqb + i, 0)),
                  pl.BlockSpec((128, n_cmp_pad), lambda b, i: (0, 0))],
        out_specs=pl.BlockSpec((tq, NSA_HEADS * 128), lambda b, i: (b * nqb + i, 0)),
        scratch_shapes=[pltpu.VMEM((n_cmp_pad, 128), MXU_DTYPE), pltpu.VMEM((128, n_cmp_pad), MXU_DTYPE),
                        rows, trans, rows, trans, pltpu.VMEM((128, tq), F32), pltpu.VMEM((2, 1, tk, cols), F32),
                        stat, stat, acc, stat, stat, acc],
        compiler_params=_cparams("arbitrary", "arbitrary"),
        name="nsa_prompt",
    )(nq, kvc, n4, nwin, ng, wselt)


def _merge_kernel(x_ref, oa_ref, ob_ref, oc_ref, g_ref, sh_ref, sc_ref, gt_ref,
                  wbg_ref, wa_ref, wb_ref, wc_ref, wo_ref, o_ref):
    x = x_ref[...]
    h = _mod_norm(x, g_ref[0], sh_ref[...], sc_ref[...])
    bg = _sigmoid(_dot(h, wbg_ref[0]))
    d = D_MODEL
    m = (bg[:, 0:d] * _dot(oa_ref[...], wa_ref[0]) + bg[:, d:2 * d] * _dot(ob_ref[...], wb_ref[0])
         + bg[:, 2 * d:3 * d] * _dot(oc_ref[...], wc_ref[0]))
    o_ref[...] = x + gt_ref[...] * _dot(m, wo_ref[0])


def merge(x, oa, ob, oc, layer, g_mix, mod, wbg, wa, wb, wc, wo, *, tm, rows_per_batch, per_token):
    r = x.shape[0]
    nt = rows_per_batch // tm
    row = lambda w: pl.BlockSpec((tm, w), lambda i: (i, 0))
    wspec = lambda a: pl.BlockSpec((1,) + a.shape[1:], lambda i: (layer, 0, 0))
    return pl.pallas_call(
        _merge_kernel,
        out_shape=jax.ShapeDtypeStruct((r, D_MODEL), F32),
        grid=(r // tm,),
        in_specs=[row(D_MODEL), row(oa.shape[1]), row(ob.shape[1]), row(oc.shape[1]),
                  pl.BlockSpec((1, 1, D_MODEL), lambda i: (layer, 0, 0)),
                  _mod_spec(per_token, tm, nt, 0), _mod_spec(per_token, tm, nt, 1), _mod_spec(per_token, tm, nt, 2),
                  wspec(wbg), wspec(wa), wspec(wb), wspec(wc), wspec(wo)],
        out_specs=row(D_MODEL),
        compiler_params=_cparams("arbitrary"),
        name="merge",
    )(x, oa, ob, oc, g_mix, mod, mod, mod, wbg, wa, wb, wc, wo)


def _route(h, rw_ref, rb_ref):
    logits = jnp.dot(h, rw_ref[...], precision=lax.Precision.HIGHEST, preferred_element_type=F32)
    aff = _sigmoid(logits)
    biased = aff + rb_ref[...]
    lane = _lane(aff.shape)
    grp = lax.shift_right_logical(lane, int(math.log2(EXPERTS_PER_GROUP)))
    sentinel = jnp.int32(aff.shape[-1])
    best = None
    for gi in range(N_GROUPS):
        v1 = jnp.where(grp == gi, biased, -jnp.inf)
        m1 = jnp.max(v1, axis=-1, keepdims=True)
        i1 = jnp.min(jnp.where(v1 == m1, lane, sentinel), axis=-1, keepdims=True)
        v2 = jnp.where(lane == i1, -jnp.inf, v1)
        m2 = jnp.max(v2, axis=-1, keepdims=True)
        i2 = jnp.min(jnp.where(v2 == m2, lane, sentinel), axis=-1, keepdims=True)
        cand = (m1 + m2, i1, i2)
        if best is None:
            best = cand
        else:
            better = cand[0] > best[0]
            best = tuple(jnp.where(better, c, b) for c, b in zip(cand, best))
    _, i1, i2 = best
    w1 = jnp.sum(jnp.where(lane == i1, aff, 0.0), axis=-1, keepdims=True)
    w2 = jnp.sum(jnp.where(lane == i2, aff, 0.0), axis=-1, keepdims=True)
    tot = w1 + w2
    return jnp.where(lane == i1, w1 / tot, 0.0) + jnp.where(lane == i2, w2 / tot, 0.0)


def _moe_kernel(x_ref, g_ref, sh_ref, sc_ref, gt_ref, rw_ref, rb_ref, wg_ref, wu_ref, wd_ref, gf_ref,
                o_ref, hb, gate, acc, *, final):
    e = pl.program_id(1)

    @pl.when(e == 0)
    def _():
        h = _mod_norm(x_ref[...], g_ref[0], sh_ref[...], sc_ref[...])
        hb[...] = h.astype(hb.dtype)
        gate[...] = _route(h, rw_ref, rb_ref)
        acc[...] = jnp.zeros_like(acc)

    h = hb[...]
    gv = gate[...]
    ge = jnp.sum(jnp.where(_lane(gv.shape) == e, gv, 0.0), axis=-1, keepdims=True)
    hid = _silu(_dot(h, wg_ref[0, 0])) * _dot(h, wu_ref[0, 0]) * ge
    acc[...] += _dot(hid, wd_ref[0, 0])

    @pl.when(e == N_EXPERTS - 1)
    def _():
        y = x_ref[...] + gt_ref[...] * acc[...]
        if final:
            y = _rms(y) * gf_ref[...]
        o_ref[...] = y


def moe(x, layer, g_ffn, mod, rw, rb, wg, wu, wd, g_final, *, tm, rows_per_batch, per_token, final):
    r = x.shape[0]
    nt = rows_per_batch // tm
    row = pl.BlockSpec((tm, D_MODEL), lambda i, e: (i, 0))
    if per_token:
        mspec = lambda c: pl.BlockSpec((tm, D_MODEL), lambda i, e: (i, c))
    else:
        mspec = lambda c: pl.BlockSpec((None, 1, D_MODEL), lambda i, e: (i // nt, 0, c))
    kern = functools.partial(_moe_kernel, final=final)
    return pl.pallas_call(
        kern,
        out_shape=jax.ShapeDtypeStruct((r, D_MODEL), F32),
        grid=(r // tm, N_EXPERTS),
        in_specs=[row, pl.BlockSpec((1, 1, D_MODEL), lambda i, e: (layer, 0, 0)),
                  mspec(3), mspec(4), mspec(5),
                  pl.BlockSpec((D_MODEL, 128), lambda i, e: (0, 0)),
                  pl.BlockSpec((1, 128), lambda i, e: (0, 0)),
                  pl.BlockSpec((1, 1, D_MODEL, EXPERT_FF), lambda i, e: (layer, e, 0, 0)),
                  pl.BlockSpec((1, 1, D_MODEL, EXPERT_FF), lambda i, e: (layer, e, 0, 0)),
                  pl.BlockSpec((1, 1, EXPERT_FF, D_MODEL), lambda i, e: (layer, e, 0, 0)),
                  pl.BlockSpec((1, D_MODEL), lambda i, e: (0, 0))],
        out_specs=row,
        scratch_shapes=[pltpu.VMEM((tm, D_MODEL), MXU_DTYPE), pltpu.VMEM((tm, 128), F32),
                        pltpu.VMEM((tm, D_MODEL), F32)],
        compiler_params=_cparams("arbitrary", "arbitrary"),
        name="moe",
    )(x, g_ffn, mod, mod, mod, rw, rb, wg, wu, wd, g_final)


def _page_specs(n_pages, pages_per_step, layer, block):
    zeros = (0,) * len(block)

    def spec(p):
        return pl.BlockSpec((None, None) + block,
                            lambda b, j, pt: (layer, pt[b * n_pages + j * pages_per_step + p]) + zeros)
    return [spec(p) for p in range(pages_per_step)]


def _new_token_mask(rows):
    t = lax.broadcasted_iota(jnp.int32, (rows, 8), 0) & (N_NEW - 1)
    tk = _lane((rows, 8))
    return jnp.where(tk <= t, jnp.where(tk < N_NEW, 1, 0), 0) > 0


def _diff_sample_kernel(pt_ref, q_ref, *rest, n_pg, lam_init):
    pages = rest[:n_pg]
    new_ref, lam_ref, g_ref, o_ref, m_ref, l_ref, a_ref = rest[n_pg:]
    j = pl.program_id(1)
    nh = DIFF_HEADS

    @pl.when(j == 0)
    def _():
        m_ref[...] = jnp.full(m_ref.shape, NEG, F32)
        l_ref[...] = jnp.zeros(l_ref.shape, F32)
        a_ref[...] = jnp.zeros(a_ref.shape, F32)

    qh = [q_ref[h] for h in range(nh)]

    def keys(pg, h):
        return pg[pl.ds(h, PAGE_SIZE, stride=2 * nh), :].astype(MXU_DTYPE)

    def values(pg, h):
        return pg[pl.ds(nh + h, PAGE_SIZE, stride=2 * nh), :].astype(MXU_DTYPE)

    s = jnp.concatenate([jnp.concatenate([_dot_nt(qh[h], keys(pg, h)) for h in range(nh)], axis=0) for pg in pages],
                        axis=1)

    def weighted_values(p):
        acc = None
        for pi, pg in enumerate(pages):
            pp = p[:, pi * PAGE_SIZE:(pi + 1) * PAGE_SIZE]
            part = jnp.concatenate([_dot(pp[8 * h:8 * h + 8], values(pg, h)) for h in range(nh)], axis=0)
            acc = part if acc is None else acc + part
        return acc

    _flash_next(s, weighted_values, m_ref, l_ref, a_ref)

    @pl.when(j == pl.num_programs(1) - 1)
    def _():
        new = new_ref[...]
        s = jnp.concatenate([_dot_nt(qh[h], new[:, 128 * h:128 * h + 128]) for h in range(nh)], axis=0)
        s = jnp.where(_new_token_mask(8 * nh), s, NEG)
        _flash_next(s, lambda p: jnp.concatenate(
            [_dot(p[8 * h:8 * h + 8], new[:, 512 + 128 * h:640 + 128 * h]) for h in range(nh)], axis=0),
            m_ref, l_ref, a_ref)
        a = a_ref[...] / l_ref[...]
        lam = _diff_lambda(lam_ref, lam_init)
        outs = []
        for h in range(nh):
            grp = a[8 * h:8 * h + 8]
            o = grp - lam * pltpu.roll(grp, 4, 0)
            outs.append(_rms(o) * g_ref[0] * (1.0 - lam_init))
        o_ref[...] = jnp.concatenate(outs, axis=1).astype(o_ref.dtype)


def diff_sample(page_table, qrows, cache, new_kv, diff_lambda, diff_norm_g, layer, *, pages_per_step=16):
    n_seq, n_pages = page_table.shape
    lam_init = 0.8 - 0.6 * math.exp(-0.3 * layer)
    kern = functools.partial(_diff_sample_kernel, n_pg=pages_per_step, lam_init=lam_init)
    rows = 8 * DIFF_HEADS
    grid_spec = pltpu.PrefetchScalarGridSpec(
        num_scalar_prefetch=1,
        grid=(n_seq, n_pages // pages_per_step),
        in_specs=[pl.BlockSpec((None, DIFF_HEADS, 8, 128), lambda b, j, pt: (b, 0, 0, 0))]
        + _page_specs(n_pages, pages_per_step, layer, (2 * DIFF_HEADS * PAGE_SIZE, 128))
        + [pl.BlockSpec((None, 8, 1024), lambda b, j, pt: (b, 0, 0)),
           pl.BlockSpec((1, 4, HEAD_DIM), lambda b, j, pt: (layer, 0, 0)),
           pl.BlockSpec((1, 1, DIFF_VDIM), lambda b, j, pt: (layer, 0, 0))],
        out_specs=pl.BlockSpec((None, 8, 512), lambda b, j, pt: (b, 0, 0)),
        scratch_shapes=[pltpu.VMEM((rows, 1), F32), pltpu.VMEM((rows, 1), F32), pltpu.VMEM((rows, DIFF_VDIM), F32)],
    )
    return pl.pallas_call(
        kern,
        out_shape=jax.ShapeDtypeStruct((n_seq, 8, 512), MXU_DTYPE),
        grid_spec=grid_spec,
        compiler_params=_cparams("arbitrary", "arbitrary"),
        name="diff_sample",
    )(page_table.reshape(-1), qrows, *([cache] * pages_per_step), new_kv, diff_lambda, diff_norm_g)


def _moba_sample_kernel(pt_ref, q_ref, *rest, n_pg, n_blk):
    pages = rest[:n_pg]
    nk_ref, nv_ref, o_ref, m_all, l_all, acc_all, kmean = rest[n_pg:]
    j = pl.program_id(1)
    nh = MOBA_HEADS

    @pl.when(j == 0)
    def _():
        m_all[...] = jnp.full(m_all.shape, NEG, F32)
        l_all[...] = jnp.zeros(l_all.shape, F32)
        kmean[...] = jnp.zeros(kmean.shape, F32)

    width = nh * HEAD_DIM
    q = q_ref[...]
    lane = _lane(m_all.shape)
    klane = _lane(kmean.shape)
    own_head = lax.shift_right_logical(_lane((8 * nh, width)), 6) == lax.shift_right_logical(_lane((8 * nh, width), 0), 3)
    for b2 in range(n_pg // 2):
        n = j * (n_pg // 2) + b2
        pa, pb = pages[2 * b2], pages[2 * b2 + 1]
        kt = jnp.concatenate([pa[0].reshape(width, PAGE_SIZE), pb[0].reshape(width, PAGE_SIZE)], axis=1)
        vt = jnp.concatenate([pa[1].reshape(width, PAGE_SIZE), pb[1].reshape(width, PAGE_SIZE)], axis=1)
        kmean[...] = jnp.where(klane == n, jnp.sum(kt, axis=1, keepdims=True) * (1.0 / MOBA_BLOCK), kmean[...])
        s = _dot(q, kt)
        m = jnp.max(s, axis=-1, keepdims=True)
        p = jnp.exp2(s - m)
        m_all[...] = jnp.where(lane == n, m, m_all[...])
        l_all[...] = jnp.where(lane == n, jnp.sum(p, axis=-1, keepdims=True), l_all[...])
        acc_all[n] = jnp.where(own_head, _dot_nt(p, vt), 0.0)

    @pl.when(j == pl.num_programs(1) - 1)
    def _():
        gate = _dot(q, kmean[...])
        sel = jnp.where(lane < n_blk, _top_k_mask(gate, MOBA_TOPK, n_blk), 0.0) > 0.5
        s_new = jnp.where(_new_token_mask(8 * nh), _dot_nt(q, nk_ref[...]), NEG)
        m_new = jnp.max(s_new, axis=-1, keepdims=True)
        p_new = jnp.exp2(s_new - m_new)
        m_tot = jnp.maximum(m_new, jnp.max(jnp.where(sel, m_all[...], NEG), axis=-1, keepdims=True))
        w = jnp.where(sel, jnp.exp2(m_all[...] - m_tot), 0.0)
        w_new = jnp.exp2(m_new - m_tot)
        l_tot = jnp.sum(w * l_all[...], axis=-1, keepdims=True) + w_new * jnp.sum(p_new, axis=-1, keepdims=True)
        a_new = jnp.where(own_head, _dot(p_new, nv_ref[...]), 0.0)

        def body(n, acc):
            wn = jnp.sum(jnp.where(lane == n, w, 0.0), axis=-1, keepdims=True)
            return acc + wn * acc_all[n]

        o = lax.fori_loop(0, n_blk, body, w_new * a_new) / l_tot
        o_ref[...] = (o[0:8] + o[8:16] + o[16:24] + o[24:32]).astype(o_ref.dtype)


def moba_sample(page_table, qrows, cache, new_k, new_v, layer, *, pages_per_step=32):
    n_seq, n_pages = page_table.shape
    n_blk = n_pages * PAGE_SIZE // MOBA_BLOCK
    nh = MOBA_HEADS
    width = nh * HEAD_DIM
    kern = functools.partial(_moba_sample_kernel, n_pg=pages_per_step, n_blk=n_blk)
    new_tok = pl.BlockSpec((None, 8, width), lambda b, j, pt: (b, 0, 0))
    grid_spec = pltpu.PrefetchScalarGridSpec(
        num_scalar_prefetch=1,
        grid=(n_seq, n_pages // pages_per_step),
        in_specs=[pl.BlockSpec((None, 8 * nh, width), lambda b, j, pt: (b, 0, 0))]
        + _page_specs(n_pages, pages_per_step, layer, (2, nh, HEAD_DIM, PAGE_SIZE))
        + [new_tok, new_tok],
        out_specs=new_tok,
        scratch_shapes=[pltpu.VMEM((8 * nh, LANES), F32), pltpu.VMEM((8 * nh, LANES), F32),
                        pltpu.VMEM((n_blk, 8 * nh, width), F32), pltpu.VMEM((width, LANES), F32)],
    )
    return pl.pallas_call(
        kern,
        out_shape=jax.ShapeDtypeStruct((n_seq, 8, width), MXU_DTYPE),
        grid_spec=grid_spec,
        compiler_params=_cparams("arbitrary", "arbitrary"),
        name="moba_sample",
    )(page_table.reshape(-1), qrows, *([cache] * pages_per_step), new_k, new_v)


def _nsa_sample_kernel(pt_ref, qa_ref, *rest, n_pg, past, n_cmp):
    pages = rest[:n_pg]
    (win_ref, new_ref, ng_ref, pos_ref, w1_ref, w2_ref, wsel_ref, o_ref, nwin_ref,
     xs, m_all, l_all, acc_all) = rest[n_pg:]
    j = pl.program_id(1)
    n_selp = m_all.shape[1]

    @pl.when(j == 0)
    def _():
        xs[past:past + CMP_LEN, :] = jnp.zeros((CMP_LEN, 128), F32)
        m_all[...] = jnp.full(m_all.shape, NEG, F32)
        l_all[...] = jnp.zeros(l_all.shape, F32)

    qa = qa_ref[...]
    q64 = qa[:, 0:HEAD_DIM]
    lane_b = _lane(m_all.shape)
    low = _lane((16, PAGE_SIZE)) < SEL_BLOCK
    for pi, pg in enumerate(pages):
        row0 = pl.multiple_of((j * n_pg + pi) * PAGE_SIZE, PAGE_SIZE)
        xs[pl.ds(row0, PAGE_SIZE), :] = jnp.concatenate([pg[0], pg[1]], axis=0).T
        s = _dot(q64, pg[2])
        m_lo = jnp.max(jnp.where(low, s, NEG), axis=-1, keepdims=True)
        m_hi = jnp.max(jnp.where(low, NEG, s), axis=-1, keepdims=True)
        p = jnp.exp2(s - jnp.where(low, m_lo, m_hi))
        p_lo, p_hi = jnp.where(low, p, 0.0), jnp.where(low, 0.0, p)
        n0 = 2 * (j * n_pg + pi)
        m_all[...] = jnp.where(lane_b == n0, m_lo, jnp.where(lane_b == n0 + 1, m_hi, m_all[...]))
        l_all[...] = jnp.where(lane_b == n0, jnp.sum(p_lo, axis=-1, keepdims=True),
                               jnp.where(lane_b == n0 + 1, jnp.sum(p_hi, axis=-1, keepdims=True), l_all[...]))
        pv = _dot_nt(jnp.concatenate([p_lo, p_hi], axis=0), pg[3])
        acc_all[n0] = pv[0:16]
        acc_all[n0 + 1] = pv[16:32]

    @pl.when(j == pl.num_programs(1) - 1)
    def _():
        new = new_ref[...]
        t16 = lax.broadcasted_iota(jnp.int32, (16, 1), 0) & (N_NEW - 1)
        new_ok = _new_token_mask(16)
        value = lambda o: pltpu.roll(o, HEAD_DIM, 1)[:, 0:HEAD_DIM]

        kvc = _compress_rows(xs, n_cmp + 1, pos_ref, w1_ref, w2_ref).astype(MXU_DTYPE)
        s = _dot_nt(qa, kvc)
        n = _lane(s.shape)
        vis = jnp.where(n * CMP_STRIDE + (CMP_LEN - 1) <= past + t16, n, n_cmp) < n_cmp
        s = jnp.where(vis, s, NEG)
        e = jnp.where(vis, jnp.exp2(s - jnp.max(s, axis=-1, keepdims=True)), 0.0)
        den = jnp.sum(e, axis=-1, keepdims=True)
        p = (e / jnp.where(den > 0, den, 1.0)).astype(MXU_DTYPE)
        o_cmp = value(_dot(p, kvc))
        pw = _dot(p, wsel_ref[...])
        imp = pw[0:8] + pw[8:16]
        imp = imp + pltpu.roll(imp, 4, 0)

        own = lax.shift_right_logical(past + (lax.broadcasted_iota(jnp.int32, (8, 1), 0) & (N_NEW - 1)),
                                      int(math.log2(SEL_BLOCK)))
        jb = _lane(imp.shape)
        forced = jnp.where(jb == 0, 1, jnp.where(jb == own, 1, jnp.where(jb == own - 1, 1, 0))) > 0
        impm = jnp.where(jb <= own, jnp.where(forced, jnp.inf, imp), -jnp.inf)
        selm = jnp.where(jb <= own, _top_k_mask(impm, SEL_TOPK, imp.shape[-1]), 0.0)
        sel = jnp.concatenate([selm, selm], axis=0)[:, 0:n_selp] > 0.5

        s_new = jnp.where(new_ok, _dot_nt(qa, new[:, 128:256]), NEG)
        m_new = jnp.max(s_new, axis=-1, keepdims=True)
        p_new = jnp.exp2(s_new - m_new)
        m_tot = jnp.maximum(m_new, jnp.max(jnp.where(sel, m_all[...], NEG), axis=-1, keepdims=True))
        w = jnp.where(sel, jnp.exp2(m_all[...] - m_tot), 0.0)
        w_new = jnp.exp2(m_new - m_tot)
        l_tot = jnp.sum(w * l_all[...], axis=-1, keepdims=True) + w_new * jnp.sum(p_new, axis=-1, keepdims=True)

        def body(nb, acc):
            wn = jnp.sum(jnp.where(lane_b == nb, w, 0.0), axis=-1, keepdims=True)
            return acc + wn * acc_all[nb]

        o_sel = lax.fori_loop(0, n_selp, body, w_new * value(_dot(p_new, new[:, 128:256]))) / l_tot

        win = win_ref[...]
        keep = win.shape[0]
        s_w = _dot_nt(qa, win)
        s_w = jnp.where(_lane(s_w.shape) > t16 + (keep - WINDOW), s_w, NEG)
        s_wn = jnp.where(new_ok, _dot_nt(qa, new[:, 256:384]), NEG)
        m_w = jnp.maximum(jnp.max(s_w, axis=-1, keepdims=True), jnp.max(s_wn, axis=-1, keepdims=True))
        p_w, p_wn = jnp.exp2(s_w - m_w), jnp.exp2(s_wn - m_w)
        l_w = jnp.sum(p_w, axis=-1, keepdims=True) + jnp.sum(p_wn, axis=-1, keepdims=True)
        o_win = value(_dot(p_w, win) + _dot(p_wn, new[:, 256:384])) / l_w

        g = ng_ref[...]
        top_rows = lax.broadcasted_iota(jnp.int32, (8, 1), 0) < 4

        def gate(c):
            col = lambda h: g[:, 3 * h + c:3 * h + c + 1]
            return jnp.concatenate([jnp.where(top_rows, col(0), col(1)), jnp.where(top_rows, col(2), col(3))], axis=0)

        o_ref[...] = (gate(0) * o_cmp + gate(1) * o_sel + gate(2) * o_win).astype(o_ref.dtype)

        nwin_ref[0:keep - N_NEW, :] = win_ref[N_NEW:keep, :]
        nwin_ref[keep - N_NEW:keep, :] = new[0:N_NEW, 256:384]


def nsa_sample(page_table, qa, cache, win_state, new_rows, ng8, posrows, w1p, w2p, layer, *, pages_per_step=32):
    n_seq, n_pages = page_table.shape
    past = n_pages * PAGE_SIZE
    keep = win_state.shape[2]
    total = past + N_NEW
    n_cmp = (total - CMP_LEN) // CMP_STRIDE + 1
    n_selp = past // SEL_BLOCK
    n_sel_pad = -(-(n_selp + 1) // LANES) * LANES
    wsel = jnp.asarray(_cmp_to_sel_weights(n_cmp, n_cmp + 1, n_sel_pad), MXU_DTYPE)
    kern = functools.partial(_nsa_sample_kernel, n_pg=pages_per_step, past=past, n_cmp=n_cmp)
    const = lambda shape: pl.BlockSpec(shape, lambda b, j, pt: (0,) * len(shape))
    rows = N_NEW * NSA_HEADS
    grid_spec = pltpu.PrefetchScalarGridSpec(
        num_scalar_prefetch=1,
        grid=(n_seq, n_pages // pages_per_step),
        in_specs=[pl.BlockSpec((None, rows, 128), lambda b, j, pt: (b, 0, 0))]
        + _page_specs(n_pages, pages_per_step, layer, (4, HEAD_DIM, PAGE_SIZE))
        + [pl.BlockSpec((None, None, keep, 128), lambda b, j, pt: (layer, b, 0, 0)),
           pl.BlockSpec((None, 8, 384), lambda b, j, pt: (b, 0, 0)),
           pl.BlockSpec((None, 8, 128), lambda b, j, pt: (b, 0, 0)),
           pl.BlockSpec((1, CMP_LEN, 1, 128), lambda b, j, pt: (layer, 0, 0, 0)),
           pl.BlockSpec((1, CMP_LEN // 2, 256, 512), lambda b, j, pt: (layer, 0, 0, 0)),
           pl.BlockSpec((1, 512, 128), lambda b, j, pt: (layer, 0, 0)),
           const((n_cmp + 1, n_sel_pad))],
        out_specs=(pl.BlockSpec((None, rows, HEAD_DIM), lambda b, j, pt: (b, 0, 0)),
                   pl.BlockSpec((None, keep, 128), lambda b, j, pt: (b, 0, 0))),
        scratch_shapes=[pltpu.VMEM((past + CMP_LEN, 128), F32), pltpu.VMEM((rows, n_selp), F32),
                        pltpu.VMEM((rows, n_selp), F32), pltpu.VMEM((n_selp, rows, HEAD_DIM), F32)],
    )
    return pl.pallas_call(
        kern,
        out_shape=(jax.ShapeDtypeStruct((n_seq, rows, HEAD_DIM), MXU_DTYPE),
                   jax.ShapeDtypeStruct((n_seq, keep, 128), F32)),
        grid_spec=grid_spec,
        compiler_params=_cparams("arbitrary", "arbitrary"),
        name="nsa_sample",
    )(page_table.reshape(-1), qa, *([cache] * pages_per_step), win_state, new_rows, ng8, posrows, w1p, w2p, wsel)


def sample_operands(dq, dkv, mq, mkv, nq, n4, nwin, ng, *, n_seq):
    per_seq = lambda a: a.astype(F32).reshape(n_seq, N_NEW, a.shape[-1])
    pad8 = lambda a: jnp.pad(a, [(0, 0)] * (a.ndim - 2) + [(0, 8 - N_NEW), (0, 0)])
    heads_first = lambda a, nh: a.reshape(n_seq, N_NEW, nh, a.shape[-1] // nh).transpose(0, 2, 1, 3)
    qd = heads_first(per_seq(dq), DIFF_HEADS)
    map_ok = jnp.asarray((np.arange(128) // HEAD_DIM)[None, :] == np.arange(2)[:, None], F32)
    diff_q = (qd[:, :, None, :, :] * map_ok[None, None, :, None, :]).reshape(n_seq, DIFF_HEADS, 8, 128)
    mkv_s = per_seq(mkv)
    head_ok = jnp.asarray((np.arange(256) // HEAD_DIM)[None, :] == np.arange(MOBA_HEADS)[:, None], F32)
    qn = heads_first(per_seq(nq), NSA_HEADS).reshape(n_seq, N_NEW * NSA_HEADS, HEAD_DIM)
    g = per_seq(ng)
    return {
        "diff_q": diff_q.astype(MXU_DTYPE), "diff_new": pad8(per_seq(dkv)),
        "moba_q": pad8(per_seq(mq)[:, None, :, :] * head_ok[None, :, None, :]).reshape(n_seq, 8 * MOBA_HEADS, 256)
        .astype(MXU_DTYPE),
        "moba_new_k": pad8(mkv_s[..., 0:256]), "moba_new_v": pad8(mkv_s[..., 256:512]),
        "nsa_qa": jnp.concatenate([qn, jnp.zeros_like(qn)], axis=-1).astype(MXU_DTYPE),
        "nsa_new": pad8(jnp.concatenate([per_seq(n4), per_seq(nwin)], axis=-1)),
        "nsa_g": jnp.concatenate([g, g], axis=1),
    }


def _tokens_first(o, n_seq, nh, rows_per_head):
    o = o.reshape(n_seq, nh, rows_per_head, o.shape[-1])[:, :, :N_NEW]
    return o.transpose(0, 2, 1, 3).reshape(n_seq * N_NEW, nh * o.shape[-1])


def kernel(x_prompt, x_sample, cache_diff_kv, cache_moba_kv, cache_nsa_kv, state_nsa_win, page_table, c_prompt, c_sample, w_ada, b_ada, g_mix, g_ffn, w_in, diff_lambda, diff_norm_g, cmp_pos, cmp_w1, cmp_w2, w_br_a, w_br_b, w_br_c, w_out, router_w, router_b, w_e_gate, w_e_up, w_e_down, g_final):
    batch, seq, d = x_prompt.shape
    n_seq, n_new, _ = x_sample.shape
    assert n_new == N_NEW and d == D_MODEL and seq % MOBA_BLOCK == 0
    depth = w_ada.shape[0]
    n_phys = cache_diff_kv.shape[1]
    past = page_table.shape[1] * PAGE_SIZE
    keep = state_nsa_win.shape[2]
    rows_s = n_seq * n_new

    n_cond = batch + n_seq
    c_all = jnp.concatenate([c_prompt, c_sample, jnp.zeros((-n_cond % 8, d), F32)], axis=0)
    mod = ada_ln(c_all, w_ada, b_ada)
    w1 = jnp.pad(w_in[:, :, :PROJ_COLS], ((0, 0), (0, 0), (0, PROJ_PAD - PROJ_COLS))).astype(MXU_DTYPE)
    wbg = w_in[:, :, PROJ_COLS:].astype(MXU_DTYPE)
    wa, wb, wo = w_br_a.astype(MXU_DTYPE), w_br_b.astype(MXU_DTYPE), w_out.astype(MXU_DTYPE)
    wc4 = w_br_c.reshape(depth, NSA_HEADS, HEAD_DIM, d)
    wc_wide = jnp.concatenate([jnp.zeros_like(wc4), wc4], axis=2).reshape(depth, NSA_HEADS * 128, d).astype(MXU_DTYPE)
    wc = w_br_c.astype(MXU_DTYPE)
    rw = jnp.pad(router_w, ((0, 0), (0, LANES - N_EXPERTS)))
    rb = jnp.pad(router_b, (0, LANES - N_EXPERTS)).reshape(1, LANES)
    wg, wu, wd = w_e_gate.astype(MXU_DTYPE), w_e_up.astype(MXU_DTYPE), w_e_down.astype(MXU_DTYPE)
    posrows, w1p, w2p = compress_weights(cmp_pos, cmp_w1, cmp_w2)
    g_mix3, g_ffn3 = g_mix.reshape(depth, 1, d), g_ffn.reshape(depth, 1, d)
    gf = g_final.reshape(1, d)
    norm_g_row = diff_norm_g.reshape(depth, 1, DIFF_VDIM)
    norm_g_col = diff_norm_g.reshape(depth, DIFF_VDIM, 1)
    tables_p = rope_tables(jnp.arange(seq, dtype=jnp.int32))
    tables_s = rope_tables(jnp.tile(past + jnp.arange(n_new, dtype=jnp.int32), n_seq))
    cache_d = cache_diff_kv.reshape(depth, n_phys, PAGE_SIZE * 2 * DIFF_HEADS, DIFF_VDIM)
    cache_m = jnp.transpose(cache_moba_kv, (0, 1, 3, 4, 5, 2))
    cache_n = jnp.transpose(cache_nsa_kv.reshape(depth, n_phys, PAGE_SIZE, 4, HEAD_DIM), (0, 1, 3, 4, 2))
    win_state = state_nsa_win.reshape(depth, n_seq, keep, 2 * HEAD_DIM)

    xp = x_prompt.reshape(batch * seq, d)
    xs = x_sample.reshape(rows_s, d)
    prompt = dict(rows_per_batch=seq, per_token=False)
    sample = dict(tm=rows_s, rows_per_batch=rows_s, per_token=True)
    leaves = [[] for _ in range(8)]
    for l in range(depth):
        last = l == depth - 1
        modp = mod[l, :batch].reshape(batch, 1, 6 * d)
        dq, dkv, mq, mkv, nq, n4, nwin, ng = project(xp, l, g_mix3, modp, modp, w1, tables_p, tm=256, **prompt)
        oa = diff_prompt(dq, dkv, diff_lambda, norm_g_col, l, batch=batch, seq=seq)
        ob = moba_prompt(mq, mkv, batch=batch, seq=seq)
        kvc = compress_prompt(n4, posrows, w1p, w2p, l, batch=batch, seq=seq)
        oc = nsa_prompt(nq, kvc, n4, nwin, ng, batch=batch, seq=seq)
        xp = merge(xp, oa, ob, oc, l, g_mix3, modp, wbg, wa, wb, wc_wide, wo, tm=512, **prompt)
        xp = moe(xp, l, g_ffn3, modp, rw, rb, wg, wu, wd, gf, tm=1024, final=last, **prompt)
        leaves[0].append(dkv.reshape(batch, seq, 2, DIFF_HEADS, DIFF_VDIM))
        leaves[2].append(mkv.reshape(batch, seq, 2, MOBA_HEADS, HEAD_DIM))
        leaves[4].append(n4.reshape(batch, seq, 4, 1, HEAD_DIM))
        leaves[6].append(nwin.reshape(batch, seq, 2, 1, HEAD_DIM)[:, seq - min(WINDOW, seq):])
        mods = jnp.repeat(mod[l, batch:n_cond], n_new, axis=0)
        dq, dkv, mq, mkv, nq, n4, nwin, ng = project(xs, l, g_mix3, mods, mods, w1, tables_s, **sample)
        ops = sample_operands(dq, dkv, mq, mkv, nq, n4, nwin, ng, n_seq=n_seq)
        oa = diff_sample(page_table, ops["diff_q"], cache_d, ops["diff_new"], diff_lambda, norm_g_row, l)
        ob = moba_sample(page_table, ops["moba_q"], cache_m, ops["moba_new_k"], ops["moba_new_v"], l)
        oc, new_win = nsa_sample(page_table, ops["nsa_qa"], cache_n, win_state, ops["nsa_new"], ops["nsa_g"],
                                 posrows, w1p, w2p, l)
        oa = oa[:, :n_new].reshape(rows_s, DIFF_HEADS * DIFF_VDIM)
        ob = ob[:, :n_new].reshape(rows_s, MOBA_HEADS * HEAD_DIM)
        oc = _tokens_first(oc, n_seq, NSA_HEADS, N_NEW)
        xs = merge(xs, oa, ob, oc, l, g_mix3, mods, wbg, wa, wb, wc, wo, **sample)
        xs = moe(xs, l, g_ffn3, mods, rw, rb, wg, wu, wd, gf, final=last, **sample)
        leaves[1].append(dkv.reshape(n_seq, n_new, 2, DIFF_HEADS, DIFF_VDIM))
        leaves[3].append(mkv.reshape(n_seq, n_new, 2, MOBA_HEADS, HEAD_DIM))
        leaves[5].append(n4.reshape(n_seq, n_new, 4, 1, HEAD_DIM))
        leaves[7].append(new_win.reshape(n_seq, keep, 2, 1, HEAD_DIM))
    return (xp.reshape(batch, seq, d), xs.reshape(n_seq, n_new, d)) + tuple(jnp.stack(v) for v in leaves)
```
